```python
import jax
import jax.numpy as jnp
from jax import lax
import numpy as np

D_MODEL = 2048
BATCH = 4
SEQ = 4096
DEPTH = 2

GRID_W = 64
CTX_LEN = 256
HEAD_DIM = 128
NA_HEADS = D_MODEL // (2 * HEAD_DIM)
NA_KH = 8
NA_KW = 16
SW_HEADS = D_MODEL // (2 * HEAD_DIM)
SW_KV_HEADS = SW_HEADS // 4
SW_WINDOW = 128
SW_BLOCK = 128
ROPE_BASE = 10000.0
NA_WIDTH = NA_HEADS * HEAD_DIM
SW_Q_WIDTH = SW_HEADS * HEAD_DIM
SW_KV_WIDTH = SW_KV_HEADS * HEAD_DIM
IN_WIDTH = 3 * NA_WIDTH + SW_Q_WIDTH + 2 * SW_KV_WIDTH
MIX_WIDTH = NA_WIDTH + SW_Q_WIDTH
CONV_WIDTH = 31
D_FF_DENSE = 5632
N_EXPERTS = 8
TOP_K = 2
D_FF_EXPERT = 7168
N_MOD = 6
N_EVEN = (DEPTH + 1) // 2
N_ODD = DEPTH // 2
EPS = 1e-6

kernel_name = "hybrid_natten_swa_conformer_moe_dit"


def rms_norm(x, g):
    xf = x.astype(jnp.float32)
    y = xf * lax.rsqrt(jnp.mean(xf * xf, axis=-1, keepdims=True) + EPS)
    return (y * g.astype(jnp.float32)).astype(x.dtype)


def layer_norm(x, g, b):
    xf = x.astype(jnp.float32)
    mu = jnp.mean(xf, axis=-1, keepdims=True)
    var = jnp.mean(jnp.square(xf - mu), axis=-1, keepdims=True)
    y = (xf - mu) * lax.rsqrt(var + EPS)
    return (y * g.astype(jnp.float32) + b.astype(jnp.float32)).astype(x.dtype)


def ada_mod(cvec, w, b):
    m = jax.nn.silu(cvec) @ w + b
    return jnp.split(m[:, None, :], N_MOD, axis=-1)


def modulate(h, shift, scale):
    return h * (1 + scale) + shift


def rope_1d(x, pos):
    n = x.shape[-1]
    inv = ROPE_BASE ** (-jnp.arange(0, n, 2, dtype=jnp.float32) / n)
    ang = pos.astype(jnp.float32)[:, None] * inv[None, :]
    cos = jnp.cos(ang)[:, None, :]
    sin = jnp.sin(ang)[:, None, :]
    x1, x2 = x[..., : n // 2], x[..., n // 2:]
    return jnp.concatenate([x1 * cos - x2 * sin, x2 * cos + x1 * sin], axis=-1).astype(x.dtype)


def axial_rope(x, row, col):
    half = x.shape[-1] // 2
    return jnp.concatenate([rope_1d(x[..., :half], row), rope_1d(x[..., half:], col)], axis=-1)


def swiglu(t, wg, wu, wd):
    return (jax.nn.silu(t @ wg) * (t @ wu)) @ wd


def ctx_attention(q, k, v, sinks):
    b_, c_, h_, d_ = q.shape
    hk = k.shape[2]
    qg = q.reshape(b_, c_, hk, h_ // hk, d_)
    s = jnp.einsum('bqkgd,bckd->bkgqc', qg, k).astype(jnp.float32) * (d_ ** -0.5)
    if sinks is not None:
        sink = jnp.broadcast_to(sinks.astype(jnp.float32).reshape(1, hk, h_ // hk, 1, 1), s.shape[:-1] + (1,))
        s = jnp.concatenate([s, sink], axis=-1)
    p = jax.nn.softmax(s, axis=-1)[..., :c_].astype(v.dtype)
    o = jnp.einsum('bkgqc,bckd->bqkgd', p, v)
    return o.reshape(b_, c_, h_ * d_)


def neighbourhood_attention(q, k, v, k_ctx, v_ctx, rpb):
    b_, s_, h_, d_ = q.shape
    rows = s_ // GRID_W
    kh = min(NA_KH, rows)
    kw = NA_KW
    scale = d_ ** -0.5
    qg = q.reshape(b_, rows, GRID_W, h_, d_)
    kg = k.reshape(b_, rows, GRID_W, h_, d_)
    vg = v.reshape(b_, rows, GRID_W, h_, d_)
    r_idx = jnp.arange(rows)
    row_start = jnp.clip(r_idx - kh // 2, 0, rows - kh)
    c_idx = jnp.arange(GRID_W)
    col_start = jnp.clip(c_idx - kw // 2, 0, GRID_W - kw)
    col_win = col_start[:, None] + jnp.arange(kw)[None, :]
    col_bias = col_win - c_idx[:, None] + (NA_KW - 1)
    n_nb = kh * kw

    def row_block(args):
        q_r, r = args
        rs = row_start[r]
        k_rows = lax.dynamic_slice_in_dim(kg, rs, kh, axis=1)
        v_rows = lax.dynamic_slice_in_dim(vg, rs, kh, axis=1)
        k_win = k_rows[:, :, col_win]
        v_win = v_rows[:, :, col_win]
        row_bias = rs + jnp.arange(kh) - r + (NA_KH - 1)
        bias = rpb[:, row_bias[None, :, None], col_bias[:, None, :]]
        s_nb = jnp.einsum('bqhd,brqwhd->bhqrw', q_r, k_win).astype(jnp.float32) * scale
        s_nb = s_nb + bias[None].astype(jnp.float32)
        s_cx = jnp.einsum('bqhd,bchd->bhqc', q_r, k_ctx).astype(jnp.float32) * scale
        logits = jnp.concatenate([s_nb.reshape(b_, h_, GRID_W, n_nb), s_cx], axis=-1)
        p = jax.nn.softmax(logits, axis=-1).astype(v.dtype)
        p_nb = p[..., :n_nb].reshape(b_, h_, GRID_W, kh, kw)
        return (jnp.einsum('bhqrw,brqwhd->bqhd', p_nb, v_win)
                + jnp.einsum('bhqc,bchd->bqhd', p[..., n_nb:], v_ctx))

    out = lax.map(row_block, (jnp.moveaxis(qg, 1, 0), r_idx))
    return jnp.moveaxis(out, 0, 1).reshape(b_, s_, h_ * d_)


def band_attention(q, k, v, k_ctx, v_ctx, sinks):
    b_, s_, h_, d_ = q.shape
    hk = k.shape[2]
    g_ = h_ // hk
    blk = SW_BLOCK
    nb = s_ // blk
    c_ = k_ctx.shape[1]
    scale = d_ ** -0.5

    def band(t):
        tb = jnp.pad(t, ((0, 0), (blk, blk), (0, 0), (0, 0))).reshape(b_, nb + 2, blk, hk, d_)
        return jnp.concatenate([tb[:, :-2], tb[:, 1:-1], tb[:, 2:]], axis=2)

    kb, vb = band(k), band(v)
    qb = q.reshape(b_, nb, blk, hk, g_, d_)
    s = jnp.einsum('bnqkgd,bnjkd->bnkgqj', qb, kb).astype(jnp.float32) * scale
    blk_i = jnp.arange(nb)[:, None, None]
    qpos = blk_i * blk + jnp.arange(blk)[None, :, None]
    kpos = (blk_i - 1) * blk + jnp.arange(3 * blk)[None, None, :]
    valid = (jnp.abs(qpos - kpos) <= SW_WINDOW) & (kpos >= 0) & (kpos < s_)
    s = jnp.where(valid[None, :, None, None], s, -jnp.inf)
    s_cx = jnp.einsum('bnqkgd,bckd->bnkgqc', qb, k_ctx).astype(jnp.float32) * scale
    sink = jnp.broadcast_to(sinks.astype(jnp.float32).reshape(1, 1, hk, g_, 1, 1), s.shape[:-1] + (1,))
    p = jax.nn.softmax(jnp.concatenate([s, s_cx, sink], axis=-1), axis=-1).astype(v.dtype)
    nk = 3 * blk
    o = (jnp.einsum('bnkgqj,bnjkd->bnqkgd', p[..., :nk], vb)
         + jnp.einsum('bnkgqc,bckd->bnqkgd', p[..., nk:nk + c_], v_ctx))
    return o.reshape(b_, s_, h_ * d_)


def mixer_ab(h, h_ctx, w_in, rpb, sinks, w_out, with_ctx_out):
    cuts = [NA_WIDTH, 2 * NA_WIDTH, 3 * NA_WIDTH, 3 * NA_WIDTH + SW_Q_WIDTH,
            3 * NA_WIDTH + SW_Q_WIDTH + SW_KV_WIDTH]

    def project(t):
        parts = jnp.split(t @ w_in, cuts, axis=-1)
        return [u.reshape(u.shape[:-1] + (-1, HEAD_DIM)) for u in parts]

    qa, ka, va, qb, kb, vb = project(h)
    qa_c, ka_c, va_c, qb_c, kb_c, vb_c = project(h_ctx)
    t = jnp.arange(h.shape[1])
    row, col = t // GRID_W, t % GRID_W
    qb = axial_rope(qb, row, col)
    kb = axial_rope(kb, row, col)
    y_a = neighbourhood_attention(qa, ka, va, ka_c, va_c, rpb)
    y_b = band_attention(qb, kb, vb, kb_c, vb_c, sinks)
    y = jnp.concatenate([y_a, y_b], axis=-1) @ w_out
    if with_ctx_out:
        y_ctx = jnp.concatenate([ctx_attention(qa_c, ka_c, va_c, None),
                                 ctx_attention(qb_c, kb_c, vb_c, sinks)], axis=-1) @ w_out
        return y, y_ctx
    return y, None


def conformer_conv(h, w_pw1, b_pw1, w_dw, b_dw, ln_g, ln_b, w_pw2, b_pw2):
    u = h @ w_pw1 + b_pw1
    a, g = jnp.split(u, 2, axis=-1)
    u = a * jax.nn.sigmoid(g)
    pad = CONV_WIDTH // 2
    u = lax.conv_general_dilated(u, w_dw[:, None, :], window_strides=(1,), padding=[(pad, pad)],
                                 dimension_numbers=('NWC', 'WIO', 'NWC'),
                                 feature_group_count=u.shape[-1]) + b_dw
    u = jax.nn.silu(layer_norm(u, ln_g, ln_b))
    return u @ w_pw2 + b_pw2


def moe_swiglu(h, w_router, w_gate, w_up, w_down):
    b_, s_, d_ = h.shape
    t = h.reshape(-1, d_)
    logits = (t @ w_router).astype(jnp.float32)
    top_v, top_i = lax.top_k(logits, TOP_K)
    wts = jax.nn.softmax(top_v, axis=-1)
    combine = jnp.sum(wts[..., None] * jax.nn.one_hot(top_i, N_EXPERTS, dtype=jnp.float32), axis=1)
    y = jnp.zeros_like(t)
    for e in range(N_EXPERTS):
        y = y + combine[:, e:e + 1].astype(t.dtype) * swiglu(t, w_gate[e], w_up[e], w_down[e])
    return y.reshape(b_, s_, d_)


def setup_inputs(seed: int = 0) -> dict:
    key = jax.random.key(seed)
    ks = jax.random.split(key, 32)
    D = D_MODEL

    def nrm(k, shape, scale):
        return jax.random.normal(k, shape, jnp.float32) * scale

    return {
        "x": nrm(ks[0], (BATCH, SEQ, D), 1.0),
        "c": nrm(ks[1], (BATCH, D), 1.0),
        "ctx": nrm(ks[2], (BATCH, CTX_LEN, D), 1.0),
        "c_ctx": nrm(ks[3], (D,), 1.0),
        "ada_w": nrm(ks[4], (DEPTH, D, N_MOD * D), D ** -0.5),
        "ada_b": nrm(ks[5], (DEPTH, N_MOD * D), 0.02),
        "norm_g": 1.0 + nrm(ks[6], (DEPTH, 2, D), 0.02),
        "final_g": 1.0 + nrm(ks[7], (D,), 0.02),
        "ab_w_in": nrm(ks[8], (N_EVEN, D, IN_WIDTH), D ** -0.5),
        "ab_rpb": nrm(ks[9], (N_EVEN, NA_HEADS, 2 * NA_KH - 1, 2 * NA_KW - 1), 0.1),
        "ab_sinks": nrm(ks[10], (N_EVEN, SW_HEADS), 0.5),
        "ab_w_out": nrm(ks[11], (N_EVEN, MIX_WIDTH, D), MIX_WIDTH ** -0.5),
        "ffn_w_gate": nrm(ks[12], (N_EVEN, D, D_FF_DENSE), D ** -0.5),
        "ffn_w_up": nrm(ks[13], (N_EVEN, D, D_FF_DENSE), D ** -0.5),
        "ffn_w_down": nrm(ks[14], (N_EVEN, D_FF_DENSE, D), D_FF_DENSE ** -0.5),
        "conv_w_pw1": nrm(ks[15], (N_ODD, D, 2 * D), D ** -0.5),
        "conv_b_pw1": nrm(ks[16], (N_ODD, 2 * D), 0.02),
        "conv_w_dw": nrm(ks[17], (N_ODD, CONV_WIDTH, D), CONV_WIDTH ** -0.5),
        "conv_b_dw": nrm(ks[18], (N_ODD, D), 0.02),
        "conv_ln_g": 1.0 + nrm(ks[19], (N_ODD, D), 0.02),
        "conv_ln_b": nrm(ks[20], (N_ODD, D), 0.02),
        "conv_w_pw2": nrm(ks[21], (N_ODD, D, D), D ** -0.5),
        "conv_b_pw2": nrm(ks[22], (N_ODD, D), 0.02),
        "moe_w_router": nrm(ks[23], (N_ODD, D, N_EXPERTS), D ** -0.5),
        "moe_w_gate": nrm(ks[24], (N_ODD, N_EXPERTS, D, D_FF_EXPERT), D ** -0.5),
        "moe_w_up": nrm(ks[25], (N_ODD, N_EXPERTS, D, D_FF_EXPERT), D ** -0.5),
        "moe_w_down": nrm(ks[26], (N_ODD, N_EXPERTS, D_FF_EXPERT, D), D_FF_EXPERT ** -0.5),
    }


def reference(x, c, ctx, c_ctx, ada_w, ada_b, norm_g, final_g, ab_w_in, ab_rpb, ab_sinks, ab_w_out,
              ffn_w_gate, ffn_w_up, ffn_w_down, conv_w_pw1, conv_b_pw1, conv_w_dw, conv_b_dw,
              conv_ln_g, conv_ln_b, conv_w_pw2, conv_b_pw2, moe_w_router, moe_w_gate, moe_w_up, moe_w_down):
    x_ctx = ctx
    for i in range(DEPTH):
        need_ctx = any(j % 2 == 0 for j in range(i + 1, DEPTH))
        sh1, sc1, gt1, sh2, sc2, gt2 = ada_mod(c, ada_w[i], ada_b[i])
        g1, g2 = norm_g[i, 0], norm_g[i, 1]
        h = modulate(rms_norm(x, g1), sh1, sc1)
        if i % 2 == 0 or need_ctx:
            csh1, csc1, cgt1, csh2, csc2, cgt2 = ada_mod(c_ctx[None, :], ada_w[i], ada_b[i])
            h_ctx = modulate(rms_norm(x_ctx, g1), csh1, csc1)
        if i % 2 == 0:
            e = i // 2
            y, y_ctx = mixer_ab(h, h_ctx, ab_w_in[e], ab_rpb[e], ab_sinks[e], ab_w_out[e], need_ctx)
            ffn = lambda t: swiglu(t, ffn_w_gate[e], ffn_w_up[e], ffn_w_down[e])
        else:
            o = i // 2
            conv = lambda t: conformer_conv(t, conv_w_pw1[o], conv_b_pw1[o], conv_w_dw[o], conv_b_dw[o],
                                            conv_ln_g[o], conv_ln_b[o], conv_w_pw2[o], conv_b_pw2[o])
            y = conv(h)
            y_ctx = conv(h_ctx) if need_ctx else None
            ffn = lambda t: moe_swiglu(t, moe_w_router[o], moe_w_gate[o], moe_w_up[o], moe_w_down[o])
        x = x + gt1 * y
        x = x + gt2 * ffn(modulate(rms_norm(x, g2), sh2, sc2))
        if need_ctx:
            x_ctx = x_ctx + cgt1 * y_ctx
            x_ctx = x_ctx + cgt2 * ffn(modulate(rms_norm(x_ctx, g2), csh2, csc2))
    return rms_norm(x, final_g)
```

```python
import functools

import numpy as np
import jax
import jax.numpy as jnp
from jax import lax
from jax.experimental import pallas as pl
from jax.experimental.pallas import tpu as pltpu

F32 = jnp.float32
BF16 = jnp.bfloat16

GRID_W = 64
HEAD_DIM = 128
NA_KH = 8
NA_KW = 16
SW_WINDOW = 128
ROPE_BASE = 10000.0
CONV_WIDTH = 31
N_MOD = 6
TOP_K = 2
EPS = 1e-6
NEG = -1e30

VMEM_LIMIT = 56 * 1024 * 1024
NA_QROWS = 8
NA_KROWS = 16
CONV_HALO = 16
FFN_OUT_CHUNK = 512
SUBLANES = 8


def _pick_tile(n, candidates):
    return next(t for t in candidates if n % t == 0)


def _cparams(sem):
    return pltpu.CompilerParams(dimension_semantics=sem, vmem_limit_bytes=VMEM_LIMIT)


def _silu(v):
    return v / (1.0 + jnp.exp(-v))


def _sigmoid(v):
    return 1.0 / (1.0 + jnp.exp(-v))


def _norm_mod(x, g, sh, sc):
    ms = jnp.mean(x * x, axis=-1, keepdims=True)
    y = x * lax.rsqrt(ms + EPS) * g
    return y * (1.0 + sc) + sh


def _ada_kernel(c_ref, w_ref, b_ref, o_ref):
    s = _silu(c_ref[...]).astype(BF16)
    acc = jnp.dot(s, w_ref[...].astype(BF16), preferred_element_type=F32)
    o_ref[...] = acc + b_ref[...]


def ada_mod_all(cvec, ada_w, ada_b, tn=1024):
    depth, d, n = ada_w.shape
    rows = cvec.shape[0]
    return pl.pallas_call(
        _ada_kernel,
        grid=(depth, n // tn),
        in_specs=[
            pl.BlockSpec((rows, d), lambda l, j: (0, 0)),
            pl.BlockSpec((None, d, tn), lambda l, j: (l, 0, j)),
            pl.BlockSpec((None, 1, tn), lambda l, j: (l, 0, j)),
        ],
        out_specs=pl.BlockSpec((None, rows, tn), lambda l, j: (l, 0, j)),
        out_shape=jax.ShapeDtypeStruct((depth, rows, n), F32),
        compiler_params=_cparams(("parallel", "parallel")),
        name="ada_mod",
    )(cvec, ada_w, ada_b.reshape(depth, 1, n))


def _rope_head(v, cos, sa, sb):
    return v * cos + pltpu.roll(v, 96, 1) * sa + pltpu.roll(v, 32, 1) * sb


def _inproj_kernel(x_ref, g_ref, sh_ref, sc_ref, w_ref, cos_ref, sa_ref, sb_ref, o_ref, h_scr,
                   *, rope_lo, rope_hi, tn):
    j = pl.program_id(1)

    @pl.when(j == 0)
    def _():
        h_scr[...] = _norm_mod(x_ref[...], g_ref[...], sh_ref[...], sc_ref[...]).astype(BF16)

    acc = jnp.dot(h_scr[...], w_ref[...], preferred_element_type=F32)
    heads = tn // HEAD_DIM
    first_tile = rope_lo // tn
    last_tile = (rope_hi - 1) // tn if rope_hi > rope_lo else -1

    def plain():
        o_ref[...] = acc.astype(o_ref.dtype)

    if rope_hi <= rope_lo:
        plain()
        return

    pl.when((j < first_tile) | (j > last_tile))(plain)

    for t in range(first_tile, last_tile + 1):
        def roped(t=t):
            cos, sa, sb = cos_ref[...], sa_ref[...], sb_ref[...]
            for hh in range(heads):
                col = t * tn + hh * HEAD_DIM
                v = acc[:, hh * HEAD_DIM:(hh + 1) * HEAD_DIM]
                if rope_lo <= col < rope_hi:
                    v = _rope_head(v, cos, sa, sb)
                o_ref[:, hh * HEAD_DIM:(hh + 1) * HEAD_DIM] = v.astype(o_ref.dtype)
        pl.when(j == t)(roped)


def in_proj(x2d, g, mod, mod_row_of_tile, w_bf16, rope_tabs, rope_lo, rope_hi, seq, tm):
    r, d = x2d.shape
    n = w_bf16.shape[1]
    tn = _pick_tile(n, (512, 256, 128))
    cos, sa, sb = rope_tabs
    tiles_per_seq = seq // tm
    tab_spec = pl.BlockSpec((tm, HEAD_DIM), lambda i, j: (i % tiles_per_seq, 0))
    kern = functools.partial(_inproj_kernel, rope_lo=rope_lo, rope_hi=rope_hi, tn=tn)
    return pl.pallas_call(
        kern,
        grid=(r // tm, n // tn),
        in_specs=[
            pl.BlockSpec((tm, d), lambda i, j: (i, 0)),
            pl.BlockSpec((1, d), lambda i, j: (0, 0)),
            pl.BlockSpec((None, 1, d), lambda i, j: (mod_row_of_tile(i), 0, 0)),
            pl.BlockSpec((None, 1, d), lambda i, j: (mod_row_of_tile(i), 0, 1)),
            pl.BlockSpec((d, tn), lambda i, j: (0, j)),
            tab_spec, tab_spec, tab_spec,
        ],
        out_specs=pl.BlockSpec((tm, tn), lambda i, j: (i, j)),
        out_shape=jax.ShapeDtypeStruct((r, n), BF16),
        scratch_shapes=[pltpu.VMEM((tm, d), BF16)],
        compiler_params=_cparams(("parallel", "arbitrary")),
        name="in_proj",
    )(x2d, g.reshape(1, d), mod, mod, w_bf16, cos, sa, sb)


def _glu_kernel(x_ref, g_ref, sh_ref, sc_ref, wa_ref, wg_ref, ba_ref, bg_ref, o_ref, h_scr):
    j = pl.program_id(1)

    @pl.when(j == 0)
    def _():
        h_scr[...] = _norm_mod(x_ref[...], g_ref[...], sh_ref[...], sc_ref[...]).astype(BF16)

    h = h_scr[...]
    a = jnp.dot(h, wa_ref[...], preferred_element_type=F32) + ba_ref[...]
    gate = jnp.dot(h, wg_ref[...], preferred_element_type=F32) + bg_ref[...]
    o_ref[...] = a * _sigmoid(gate)


def pw1_glu(x2d, g, mod, seq, w_bf16, b, tm=1024, tn=512):
    r, d = x2d.shape
    half = w_bf16.shape[1] // 2
    nj = half // tn
    tiles_per_seq = seq // tm
    b2 = b.reshape(1, 2 * half)
    return pl.pallas_call(
        _glu_kernel,
        grid=(r // tm, nj),
        in_specs=[
            pl.BlockSpec((tm, d), lambda i, j: (i, 0)),
            pl.BlockSpec((1, d), lambda i, j: (0, 0)),
            pl.BlockSpec((None, 1, d), lambda i, j: (i // tiles_per_seq, 0, 0)),
            pl.BlockSpec((None, 1, d), lambda i, j: (i // tiles_per_seq, 0, 1)),
            pl.BlockSpec((d, tn), lambda i, j: (0, j)),
            pl.BlockSpec((d, tn), lambda i, j: (0, j + nj)),
            pl.BlockSpec((1, tn), lambda i, j: (0, j)),
            pl.BlockSpec((1, tn), lambda i, j: (0, j + nj)),
        ],
        out_specs=pl.BlockSpec((tm, tn), lambda i, j: (i, j)),
        out_shape=jax.ShapeDtypeStruct((r, half), F32),
        scratch_shapes=[pltpu.VMEM((tm, d), BF16)],
        compiler_params=_cparams(("parallel", "arbitrary")),
        name="pw1_glu",
    )(x2d, g.reshape(1, d), mod, mod, w_bf16, w_bf16, b2, b2)


def _proj_res_kernel(*refs, n_in, splits):
    y_refs = refs[:n_in]
    w_ref, b_ref, x_ref, gt_ref, o_ref = refs[n_in:]
    acc = None
    for y_ref, (lo, hi) in zip(y_refs, splits):
        part = jnp.dot(y_ref[...], w_ref[lo:hi, :], preferred_element_type=F32)
        acc = part if acc is None else acc + part
    o_ref[...] = x_ref[...] + gt_ref[...] * (acc + b_ref[...])


def proj_residual(ys, w_bf16, b, x2d, mod, gate_chunk, seq, tm=1024, tn=512):
    r, d = x2d.shape
    kdim, n = w_bf16.shape
    tiles_per_seq = seq // tm
    splits, lo = [], 0
    for y in ys:
        splits.append((lo, lo + y.shape[1]))
        lo += y.shape[1]
    assert lo == kdim
    chunks_per_d = d // tn
    kern = functools.partial(_proj_res_kernel, n_in=len(ys), splits=tuple(splits))
    return pl.pallas_call(
        kern,
        grid=(r // tm, n // tn),
        in_specs=[pl.BlockSpec((tm, y.shape[1]), lambda i, j: (i, 0)) for y in ys] + [
            pl.BlockSpec((kdim, tn), lambda i, j: (0, j)),
            pl.BlockSpec((1, tn), lambda i, j: (0, j)),
            pl.BlockSpec((tm, tn), lambda i, j: (i, j)),
            pl.BlockSpec((None, 1, tn),
                         lambda i, j: (i // tiles_per_seq, 0, gate_chunk * chunks_per_d + j)),
        ],
        out_specs=pl.BlockSpec((tm, tn), lambda i, j: (i, j)),
        out_shape=jax.ShapeDtypeStruct((r, n), F32),
        compiler_params=_cparams(("parallel", "parallel")),
        name="proj_residual",
    )(*ys, w_bf16, b.reshape(1, n), x2d, mod)


def _na_kernel(q_ref, k_ref, v_ref, kc_ref, vc_ref, bias_ref, o_ref, kcat, vcat, *, scale, n_win, n_blocks):
    i = pl.program_id(2)
    first_row = jnp.clip(i * NA_QROWS - (NA_KROWS - NA_QROWS) // 2, 0, (n_blocks * NA_QROWS) - NA_KROWS)
    start = pl.multiple_of(first_row * GRID_W, 256)
    kcat[0:n_win, :] = k_ref[pl.ds(start, n_win), :]
    vcat[0:n_win, :] = v_ref[pl.ds(start, n_win), :]
    kcat[n_win:, :] = kc_ref[...]
    vcat[n_win:, :] = vc_ref[...]
    s = lax.dot_general(q_ref[...], kcat[...], (((1,), (1,)), ((), ())), preferred_element_type=F32)
    s = s * scale + bias_ref[...]
    m = jnp.max(s, axis=-1, keepdims=True)
    p = jnp.exp(s - m)
    l = jnp.sum(p, axis=-1, keepdims=True)
    o = jnp.dot(p.astype(BF16), vcat[...], preferred_element_type=F32)
    o_ref[...] = (o / l).astype(o_ref.dtype)


def na_bias_table(rpb, rows, n_ctx):
    heads = rpb.shape[0]
    c = np.arange(GRID_W)
    dc = c[None, :] - c[:, None]
    col_start = np.clip(c - NA_KW // 2, 0, GRID_W - NA_KW)
    col_valid = (c[None, :] >= col_start[:, None]) & (c[None, :] < col_start[:, None] + NA_KW)
    col_idx = np.clip(dc + NA_KW - 1, 0, 2 * NA_KW - 2)
    toep = jnp.take(rpb, jnp.asarray(col_idx.reshape(-1)), axis=2).reshape(heads, 2 * NA_KH - 1, GRID_W, GRID_W)
    kh = min(NA_KH, rows)
    n_blocks = rows // NA_QROWS
    tables = []
    for q0, k0 in ((0, 0), (NA_QROWS, NA_QROWS - (NA_KROWS - NA_QROWS) // 2),
                   ((n_blocks - 1) * NA_QROWS, rows - NA_KROWS)):
        r = q0 + np.arange(NA_QROWS)
        kr = k0 + np.arange(NA_KROWS)
        rs = np.clip(r - kh // 2, 0, rows - kh)
        row_valid = (kr[None, :] >= rs[:, None]) & (kr[None, :] < rs[:, None] + kh)
        row_idx = np.clip(kr[None, :] - r[:, None] + NA_KH - 1, 0, 2 * NA_KH - 2)
        blk = jnp.take(toep, jnp.asarray(row_idx.reshape(-1)), axis=1)
        blk = blk.reshape(heads, NA_QROWS, NA_KROWS, GRID_W, GRID_W).transpose(0, 1, 3, 2, 4)
        valid = row_valid[:, None, :, None] & col_valid[None, :, None, :]
        blk = jnp.where(jnp.asarray(valid)[None], blk, NEG)
        blk = blk.reshape(heads, NA_QROWS * GRID_W, NA_KROWS * GRID_W)
        tables.append(jnp.concatenate([blk, jnp.zeros((heads, NA_QROWS * GRID_W, n_ctx), F32)], axis=-1))
    return jnp.stack(tables, axis=0)


def na_attention(qkv, qkv_ctx, bias, heads, q_blk0, k_blk0, v_blk0):
    b, s, _ = qkv.shape
    n_ctx = qkv_ctx.shape[1]
    rows = s // GRID_W
    n_blocks = rows // NA_QROWS
    tq = NA_QROWS * GRID_W
    n_win = NA_KROWS * GRID_W
    nk = n_win + n_ctx

    def variant(i):
        return jnp.where(i == 0, 0, jnp.where(i == n_blocks - 1, 2, 1))

    kern = functools.partial(_na_kernel, scale=HEAD_DIM ** -0.5, n_win=n_win, n_blocks=n_blocks)
    return pl.pallas_call(
        kern,
        grid=(b, heads, n_blocks),
        in_specs=[
            pl.BlockSpec((None, tq, HEAD_DIM), lambda bi, h, i: (bi, i, q_blk0 + h)),
            pl.BlockSpec((None, s, HEAD_DIM), lambda bi, h, i: (bi, 0, k_blk0 + h)),
            pl.BlockSpec((None, s, HEAD_DIM), lambda bi, h, i: (bi, 0, v_blk0 + h)),
            pl.BlockSpec((None, n_ctx, HEAD_DIM), lambda bi, h, i: (bi, 0, k_blk0 + h)),
            pl.BlockSpec((None, n_ctx, HEAD_DIM), lambda bi, h, i: (bi, 0, v_blk0 + h)),
            pl.BlockSpec((None, None, tq, nk), lambda bi, h, i: (variant(i), h, 0, 0)),
        ],
        out_specs=pl.BlockSpec((None, tq, HEAD_DIM), lambda bi, h, i: (bi, i, h)),
        out_shape=jax.ShapeDtypeStruct((b, s, heads * HEAD_DIM), BF16),
        scratch_shapes=[pltpu.VMEM((nk, HEAD_DIM), BF16), pltpu.VMEM((nk, HEAD_DIM), BF16)],
        compiler_params=_cparams(("parallel", "parallel", "arbitrary")),
        name="na_attention",
    )(qkv, qkv, qkv, qkv_ctx, qkv_ctx, bias)


def _band_kernel(q_ref, k_ref, v_ref, kc_ref, vc_ref, sink_ref, o_ref, kcat, vcat, *, scale, blk, group, seq):
    kvh = pl.program_id(1)
    n = pl.program_id(2)
    n_win = 3 * blk
    start = pl.multiple_of(jnp.clip((n - 1) * blk, 0, seq - n_win), blk)
    kcat[0:n_win, :] = k_ref[pl.ds(start, n_win), :]
    vcat[0:n_win, :] = v_ref[pl.ds(start, n_win), :]
    kcat[n_win:, :] = kc_ref[...]
    vcat[n_win:, :] = vc_ref[...]
    q = jnp.concatenate([q_ref[:, g * HEAD_DIM:(g + 1) * HEAD_DIM] for g in range(group)], axis=0)
    s = lax.dot_general(q, kcat[...], (((1,), (1,)), ((), ())), preferred_element_type=F32) * scale
    nk = kcat.shape[0]
    rows = group * blk
    qpos = n * blk + lax.broadcasted_iota(jnp.int32, (rows, nk), 0) % blk
    col = lax.broadcasted_iota(jnp.int32, (rows, nk), 1)
    kpos = start + col
    valid = (col >= n_win) | (jnp.abs(qpos - kpos) <= SW_WINDOW)
    s = jnp.where(valid, s, NEG)
    head = lax.broadcasted_iota(jnp.int32, (rows, 1), 0) // blk
    sink = jnp.zeros((rows, 1), F32)
    for g in range(group):
        sink = jnp.where(head == g, sink_ref[kvh * group + g], sink)
    m = jnp.maximum(jnp.max(s, axis=-1, keepdims=True), sink)
    p = jnp.exp(s - m)
    l = jnp.sum(p, axis=-1, keepdims=True) + jnp.exp(sink - m)
    o = jnp.dot(p.astype(BF16), vcat[...], preferred_element_type=F32) / l
    for g in range(group):
        o_ref[:, g * HEAD_DIM:(g + 1) * HEAD_DIM] = o[g * blk:(g + 1) * blk, :].astype(o_ref.dtype)


def band_attention(qkv, qkv_ctx, sinks, q_heads, kv_heads, q_col0, k_col0, v_col0, blk=128):
    b, s, _ = qkv.shape
    n_ctx = qkv_ctx.shape[1]
    group = q_heads // kv_heads
    gw = group * HEAD_DIM
    nk = 3 * blk + n_ctx
    kern = functools.partial(_band_kernel, scale=HEAD_DIM ** -0.5, blk=blk, group=group, seq=s)
    return pl.pallas_call(
        kern,
        grid=(b, kv_heads, s // blk),
        in_specs=[
            pl.BlockSpec((None, blk, gw), lambda bi, kv, n: (bi, n, q_col0 // gw + kv)),
            pl.BlockSpec((None, s, HEAD_DIM), lambda bi, kv, n: (bi, 0, k_col0 // HEAD_DIM + kv)),
            pl.BlockSpec((None, s, HEAD_DIM), lambda bi, kv, n: (bi, 0, v_col0 // HEAD_DIM + kv)),
            pl.BlockSpec((None, n_ctx, HEAD_DIM), lambda bi, kv, n: (bi, 0, k_col0 // HEAD_DIM + kv)),
            pl.BlockSpec((None, n_ctx, HEAD_DIM), lambda bi, kv, n: (bi, 0, v_col0 // HEAD_DIM + kv)),
            pl.BlockSpec(memory_space=pltpu.SMEM),
        ],
        out_specs=pl.BlockSpec((None, blk, gw), lambda bi, kv, n: (bi, n, kv)),
        out_shape=jax.ShapeDtypeStruct((b, s, q_heads * HEAD_DIM), BF16),
        scratch_shapes=[pltpu.VMEM((nk, HEAD_DIM), BF16), pltpu.VMEM((nk, HEAD_DIM), BF16)],
        compiler_params=_cparams(("parallel", "parallel", "arbitrary")),
        name="band_attention",
    )(qkv, qkv, qkv, qkv_ctx, qkv_ctx, sinks)


def _ffn_kernel(x_ref, g_ref, sh_ref, sc_ref, gt_ref, wg_ref, wu_ref, wd_ref, o_ref, h_scr):
    f = pl.program_id(1)

    @pl.when(f == 0)
    def _():
        h_scr[...] = _norm_mod(x_ref[...], g_ref[...], sh_ref[...], sc_ref[...]).astype(BF16)

    h = h_scr[...]
    gate = jnp.dot(h, wg_ref[...], preferred_element_type=F32)
    up = jnp.dot(h, wu_ref[...], preferred_element_type=F32)
    a = (_silu(gate) * up).astype(BF16)
    d = o_ref.shape[1]
    last = pl.num_programs(1) - 1
    for c0 in range(0, d, FFN_OUT_CHUNK):
        cs = slice(c0, c0 + FFN_OUT_CHUNK)
        part = jnp.dot(a, wd_ref[:, cs], preferred_element_type=F32)

        @pl.when(f == 0)
        def _(part=part, cs=cs):
            o_ref[:, cs] = part

        @pl.when((f > 0) & (f < last))
        def _(part=part, cs=cs):
            o_ref[:, cs] += part

        @pl.when(f == last)
        def _(part=part, cs=cs):
            o_ref[:, cs] = x_ref[:, cs] + gt_ref[:, cs] * (o_ref[:, cs] + part)


def dense_ffn(x2d, g, mod, seq, wg, wu, wd, tm=1024, tf=512):
    r, d = x2d.shape
    dff = wg.shape[1]
    tiles_per_seq = seq // tm
    row = lambda i, f: i // tiles_per_seq
    return pl.pallas_call(
        _ffn_kernel,
        grid=(r // tm, dff // tf),
        in_specs=[
            pl.BlockSpec((tm, d), lambda i, f: (i, 0), pipeline_mode=pl.Buffered(1)),
            pl.BlockSpec((1, d), lambda i, f: (0, 0)),
            pl.BlockSpec((None, 1, d), lambda i, f: (row(i, f), 0, 3)),
            pl.BlockSpec((None, 1, d), lambda i, f: (row(i, f), 0, 4)),
            pl.BlockSpec((None, 1, d), lambda i, f: (row(i, f), 0, 5)),
            pl.BlockSpec((d, tf), lambda i, f: (0, f)),
            pl.BlockSpec((d, tf), lambda i, f: (0, f)),
            pl.BlockSpec((tf, d), lambda i, f: (f, 0)),
        ],
        out_specs=pl.BlockSpec((tm, d), lambda i, f: (i, 0)),
        out_shape=jax.ShapeDtypeStruct((r, d), F32),
        scratch_shapes=[pltpu.VMEM((tm, d), BF16)],
        compiler_params=_cparams(("parallel", "arbitrary")),
        name="dense_ffn",
    )(x2d, g.reshape(1, d), mod, mod, mod, wg, wu, wd)


def _conv_kernel(prev_ref, cur_ref, next_ref, w_ref, b_ref, g_ref, beta_ref, o_ref, ubuf, cbuf, *, ts, rb, cb):
    i = pl.program_id(1)
    last = pl.num_programs(1) - 1
    d = cur_ref.shape[1]
    zeros = jnp.zeros((CONV_HALO, d), F32)

    @pl.when(i == 0)
    def _():
        ubuf[0, 0:CONV_HALO, :] = zeros

    @pl.when(i > 0)
    def _():
        ubuf[0, 0:CONV_HALO, :] = prev_ref[...]

    ubuf[0, CONV_HALO:CONV_HALO + ts, :] = cur_ref[...]

    @pl.when(i == last)
    def _():
        ubuf[0, CONV_HALO + ts:, :] = zeros

    @pl.when(i < last)
    def _():
        ubuf[0, CONV_HALO + ts:, :] = next_ref[...]

    off = CONV_HALO - CONV_WIDTH // 2
    rows_read = ts + (off + CONV_WIDTH - 1) // SUBLANES * SUBLANES
    for s in range(1, SUBLANES):
        for a in range(0, rows_read, SUBLANES):
            ubuf[s, a:a + SUBLANES, :] = ubuf[0, a + s:a + s + SUBLANES, :]

    for c0 in range(0, d, cb):
        wt = w_ref[:, c0:c0 + cb]
        bias = b_ref[:, c0:c0 + cb]

        def row_body(rt, carry, c0=c0, wt=wt, bias=bias):
            r0 = pl.multiple_of(rt * rb, rb)
            acc = jnp.zeros((rb, cb), F32) + bias
            for k in range(CONV_WIDTH):
                s, q = (off + k) % SUBLANES, (off + k) // SUBLANES
                acc = acc + ubuf[s, pl.ds(r0 + q * SUBLANES, rb), c0:c0 + cb] * wt[k:k + 1, :]
            cbuf[pl.ds(r0, rb), c0:c0 + cb] = acc
            return carry

        lax.fori_loop(0, ts // rb, row_body, 0)

    v = cbuf[...]
    mu = jnp.mean(v, axis=-1, keepdims=True)
    cen = v - mu
    var = jnp.mean(cen * cen, axis=-1, keepdims=True)
    y = cen * lax.rsqrt(var + EPS) * g_ref[...] + beta_ref[...]
    o_ref[...] = _silu(y).astype(o_ref.dtype)


def conv_ln_swish(u, w_dw, b_dw, ln_g, ln_b, ts=256, rb=32, cb=256):
    b, s, d = u.shape
    hb = ts // CONV_HALO
    n_halo = s // CONV_HALO
    kern = functools.partial(_conv_kernel, ts=ts, rb=rb, cb=cb)
    return pl.pallas_call(
        kern,
        grid=(b, s // ts),
        in_specs=[
            pl.BlockSpec((None, CONV_HALO, d), lambda bi, i: (bi, jnp.maximum(i * hb - 1, 0), 0)),
            pl.BlockSpec((None, ts, d), lambda bi, i: (bi, i, 0)),
            pl.BlockSpec((None, CONV_HALO, d), lambda bi, i: (bi, jnp.minimum((i + 1) * hb, n_halo - 1), 0)),
            pl.BlockSpec((CONV_WIDTH, d), lambda bi, i: (0, 0)),
            pl.BlockSpec((1, d), lambda bi, i: (0, 0)),
            pl.BlockSpec((1, d), lambda bi, i: (0, 0)),
            pl.BlockSpec((1, d), lambda bi, i: (0, 0)),
        ],
        out_specs=pl.BlockSpec((None, ts, d), lambda bi, i: (bi, i, 0)),
        out_shape=jax.ShapeDtypeStruct((b, s, d), BF16),
        scratch_shapes=[pltpu.VMEM((SUBLANES, ts + 2 * CONV_HALO, d), F32), pltpu.VMEM((ts, d), F32)],
        compiler_params=_cparams(("parallel", "arbitrary")),
        name="conv_ln_swish",
    )(u, u, u, w_dw, b_dw.reshape(1, d), ln_g.reshape(1, d), ln_b.reshape(1, d))


def _bf16_bits(v):
    b = pltpu.bitcast(v, jnp.uint32)
    return (b + jnp.uint32(0x7FFF) + ((b >> 16) & jnp.uint32(1))) >> 16


def _router_kernel(x_ref, g_ref, sh_ref, sc_ref, wh_ref, wl_ref, hp_ref, idx_ref, wt_ref, *, n_experts):
    h = _norm_mod(x_ref[...], g_ref[...], sh_ref[...], sc_ref[...])
    half = h.shape[1] // 2
    hp_ref[...] = (_bf16_bits(h[:, half:]) << 16) | _bf16_bits(h[:, :half])
    h_hi = h.astype(BF16)
    h_lo = (h - h_hi.astype(F32)).astype(BF16)
    wh, wl = wh_ref[...], wl_ref[...]
    logits = (jnp.dot(h_hi, wh, preferred_element_type=F32)
              + (jnp.dot(h_hi, wl, preferred_element_type=F32) + jnp.dot(h_lo, wh, preferred_element_type=F32)))
    lane = lax.broadcasted_iota(jnp.int32, logits.shape, 1)
    big = logits.shape[1]
    logits = jnp.where(lane < n_experts, logits, -jnp.inf)
    m1 = jnp.max(logits, axis=-1, keepdims=True)
    i1 = jnp.min(jnp.where(logits == m1, lane, big), axis=-1, keepdims=True)
    rest = jnp.where(lane == i1, -jnp.inf, logits)
    m2 = jnp.max(rest, axis=-1, keepdims=True)
    i2 = jnp.min(jnp.where(rest == m2, lane, big), axis=-1, keepdims=True)
    e = jnp.exp(m2 - m1)
    w1 = 1.0 / (1.0 + e)
    w2 = e / (1.0 + e)
    idx_ref[...] = jnp.where(lane == 0, i1, jnp.where(lane == 1, i2, 0))
    wt_ref[...] = jnp.where(lane == 0, w1, jnp.where(lane == 1, w2, 0.0))


def router(x2d, g, mod, seq, w_router, tm=512):
    r, d = x2d.shape
    n_experts = w_router.shape[1]
    lanes = 128
    wpad = jnp.zeros((d, lanes), F32).at[:, :n_experts].set(w_router)
    wh = wpad.astype(BF16)
    wl = (wpad - wh.astype(F32)).astype(BF16)
    tiles_per_seq = seq // tm
    kern = functools.partial(_router_kernel, n_experts=n_experts)
    hp, idx, wt = pl.pallas_call(
        kern,
        grid=(r // tm,),
        in_specs=[
            pl.BlockSpec((tm, d), lambda i: (i, 0)),
            pl.BlockSpec((1, d), lambda i: (0, 0)),
            pl.BlockSpec((None, 1, d), lambda i: (i // tiles_per_seq, 0, 3)),
            pl.BlockSpec((None, 1, d), lambda i: (i // tiles_per_seq, 0, 4)),
            pl.BlockSpec((d, lanes), lambda i: (0, 0)),
            pl.BlockSpec((d, lanes), lambda i: (0, 0)),
        ],
        out_specs=[
            pl.BlockSpec((tm, d // 2), lambda i: (i, 0)),
            pl.BlockSpec((tm, lanes), lambda i: (i, 0)),
            pl.BlockSpec((tm, lanes), lambda i: (i, 0)),
        ],
        out_shape=[
            jax.ShapeDtypeStruct((r, d // 2), jnp.uint32),
            jax.ShapeDtypeStruct((r, lanes), jnp.int32),
            jax.ShapeDtypeStruct((r, lanes), F32),
        ],
        compiler_params=_cparams(("parallel",)),
        name="router",
    )(x2d, g.reshape(1, d), mod, mod, wh, wl)
    return hp, idx[:, :TOP_K], wt[:, :TOP_K]


def _gather_kernel(src_ref, table_ref, o_ref, sem, *, tr):
    base = pl.program_id(0) * tr

    def issue(r, carry):
        pltpu.make_async_copy(table_ref.at[pl.ds(src_ref[base + r], 1)], o_ref.at[pl.ds(r, 1)], sem).start()
        return carry

    lax.fori_loop(0, tr, issue, 0)
    pltpu.make_async_copy(table_ref.at[pl.ds(0, tr)], o_ref, sem).wait()


def gather_rows(table, src, tr=512):
    r = src.shape[0]
    w = table.shape[1]
    kern = functools.partial(_gather_kernel, tr=tr)
    return pl.pallas_call(
        kern,
        grid_spec=pltpu.PrefetchScalarGridSpec(
            num_scalar_prefetch=1,
            grid=(r // tr,),
            in_specs=[pl.BlockSpec(memory_space=pl.ANY)],
            out_specs=pl.BlockSpec((tr, w), lambda i, src: (i, 0)),
            scratch_shapes=[pltpu.SemaphoreType.DMA(())],
        ),
        out_shape=jax.ShapeDtypeStruct((r, w), table.dtype),
        compiler_params=_cparams(("arbitrary",)),
        name="gather_rows",
    )(src, table)


def _expert_kernel(te_ref, nt_ref, xp_ref, rw_ref, wg_ref, wu_ref, wd_ref, o_ref, x_scr):
    i = pl.program_id(0)
    f = pl.program_id(1)
    live = i < nt_ref[0]

    @pl.when(jnp.logical_not(live) & (f == 0))
    def _():
        o_ref[...] = jnp.zeros_like(o_ref)

    @pl.when(live & (f == 0))
    def _():
        xp = xp_ref[...]
        lo = pltpu.bitcast(xp << 16, F32)
        hi = pltpu.bitcast(xp & jnp.uint32(0xFFFF0000), F32)
        half = xp.shape[1]
        x_scr[:, :half] = lo.astype(BF16)
        x_scr[:, half:] = hi.astype(BF16)

    @pl.when(live)
    def _():
        x = x_scr[...]
        gate = jnp.dot(x, wg_ref[...].astype(BF16), preferred_element_type=F32)
        up = jnp.dot(x, wu_ref[...].astype(BF16), preferred_element_type=F32)
        a = (_silu(gate) * up).astype(BF16)
        last = pl.num_programs(1) - 1
        for c0 in range(0, o_ref.shape[1], FFN_OUT_CHUNK):
            cs = slice(c0, c0 + FFN_OUT_CHUNK)
            part = jnp.dot(a, wd_ref[:, cs].astype(BF16), preferred_element_type=F32)

            @pl.when(f == 0)
            def _(part=part, cs=cs):
                o_ref[:, cs] = part

            @pl.when((f > 0) & (f < last))
            def _(part=part, cs=cs):
                o_ref[:, cs] += part

            @pl.when(f == last)
            def _(part=part, cs=cs):
                o_ref[:, cs] = (o_ref[:, cs] + part) * rw_ref[...]


def expert_ffn(xp, row_w, tile_expert, n_tiles, wg, wu, wd, tm=1024, tf=256):
    r, half = xp.shape
    d = 2 * half
    dff = wg.shape[2]
    nf = dff // tf

    def live_tile(i, nt):
        return jnp.minimum(i, nt[0] - 1)

    def wf(i, f, nt):
        return jnp.where(i < nt[0], f, nf - 1)

    return pl.pallas_call(
        _expert_kernel,
        grid_spec=pltpu.PrefetchScalarGridSpec(
            num_scalar_prefetch=2,
            grid=(r // tm, nf),
            in_specs=[
                pl.BlockSpec((tm, half), lambda i, f, te, nt: (live_tile(i, nt), 0)),
                pl.BlockSpec((tm, 1), lambda i, f, te, nt: (live_tile(i, nt), 0)),
                pl.BlockSpec((None, d, tf), lambda i, f, te, nt: (te[live_tile(i, nt)], 0, wf(i, f, nt))),
                pl.BlockSpec((None, d, tf), lambda i, f, te, nt: (te[live_tile(i, nt)], 0, wf(i, f, nt))),
                pl.BlockSpec((None, tf, d), lambda i, f, te, nt: (te[live_tile(i, nt)], wf(i, f, nt), 0)),
            ],
            out_specs=pl.BlockSpec((tm, d), lambda i, f, te, nt: (i, 0)),
            scratch_shapes=[pltpu.VMEM((tm, d), BF16)],
        ),
        out_shape=jax.ShapeDtypeStruct((r, d), F32),
        compiler_params=_cparams(("arbitrary", "arbitrary")),
        name="expert_ffn",
    )(tile_expert, n_tiles, xp, row_w, wg, wu, wd)


def _combine_kernel(pos_ref, y_ref, x_ref, gt_ref, fg_ref, o_ref, buf, sem, *, tt):
    base = pl.program_id(0) * tt * TOP_K

    def issue(r, carry):
        for k in range(TOP_K):
            pltpu.make_async_copy(y_ref.at[pl.ds(pos_ref[base + r * TOP_K + k], 1)],
                                  buf.at[k, pl.ds(r, 1)], sem).start()
        return carry

    lax.fori_loop(0, tt, issue, 0)
    for k in range(TOP_K):
        pltpu.make_async_copy(y_ref.at[pl.ds(0, tt)], buf.at[k], sem).wait()
    y = buf[0]
    for k in range(1, TOP_K):
        y = y + buf[k]
    x = x_ref[...] + gt_ref[...] * y
    ms = jnp.mean(x * x, axis=-1, keepdims=True)
    o_ref[...] = x * lax.rsqrt(ms + EPS) * fg_ref[...]


def combine_final(y_sorted, pos, x2d, mod, seq, final_g, tt=256):
    r, d = x2d.shape
    tiles_per_seq = seq // tt
    kern = functools.partial(_combine_kernel, tt=tt)
    return pl.pallas_call(
        kern,
        grid_spec=pltpu.PrefetchScalarGridSpec(
            num_scalar_prefetch=1,
            grid=(r // tt,),
            in_specs=[
                pl.BlockSpec(memory_space=pl.ANY),
                pl.BlockSpec((tt, d), lambda i, pos: (i, 0)),
                pl.BlockSpec((None, 1, d), lambda i, pos: (i // tiles_per_seq, 0, 5)),
                pl.BlockSpec((1, d), lambda i, pos: (0, 0)),
            ],
            out_specs=pl.BlockSpec((tt, d), lambda i, pos: (i, 0)),
            scratch_shapes=[pltpu.VMEM((TOP_K, tt, d), F32), pltpu.SemaphoreType.DMA(())],
        ),
        out_shape=jax.ShapeDtypeStruct((r, d), F32),
        compiler_params=_cparams(("arbitrary",)),
        name="combine_final",
    )(pos, y_sorted, x2d, mod, final_g.reshape(1, d))


def dispatch_plan(top_i, top_w, n_experts, tm):
    t = top_i.shape[0]
    flat_e = top_i.reshape(-1)
    onehot = (flat_e[:, None] == jnp.arange(n_experts)[None, :]).astype(jnp.int32)
    rank = jnp.cumsum(onehot, axis=0) - onehot
    counts = jnp.sum(onehot, axis=0)
    tiles = (counts + tm - 1) // tm
    tile_end = jnp.cumsum(tiles)
    group_start = (tile_end - tiles) * tm
    pos = jnp.sum(onehot * (group_start[None, :] + rank), axis=1)
    n_rows = (t * TOP_K // tm + n_experts) * tm
    src = jnp.zeros((n_rows,), jnp.int32).at[pos].set(jnp.arange(t * TOP_K, dtype=jnp.int32) // TOP_K)
    row_w = jnp.zeros((n_rows,), F32).at[pos].set(top_w.reshape(-1))
    tile_ids = jnp.arange(n_rows // tm)
    tile_expert = jnp.minimum(jnp.sum(tile_ids[:, None] >= tile_end[None, :], axis=1), n_experts - 1)
    return src, row_w.reshape(n_rows, 1), pos.astype(jnp.int32), tile_expert.astype(jnp.int32), \
        tile_end[-1:].astype(jnp.int32)


def rope_tables(seq):
    t = np.arange(seq)
    row, col = t // GRID_W, t % GRID_W
    quarter = HEAD_DIM // 4
    inv = ROPE_BASE ** (-np.arange(0, 2 * quarter, 2, dtype=np.float32) / (2 * quarter))
    lane = np.arange(HEAD_DIM)
    pos = np.where(lane[None, :] < HEAD_DIM // 2, row[:, None], col[:, None]).astype(np.float32)
    ang = jnp.asarray(pos) * jnp.asarray(inv[lane % quarter])[None, :]
    cos, sin = jnp.cos(ang), jnp.sin(ang)
    first = jnp.asarray(((lane // quarter) % 2 == 0)[None, :])
    return cos, jnp.where(first, -sin, 0.0), jnp.where(first, 0.0, sin)


def kernel(x, c, ctx, c_ctx, ada_w, ada_b, norm_g, final_g, ab_w_in, ab_rpb, ab_sinks, ab_w_out,
           ffn_w_gate, ffn_w_up, ffn_w_down, conv_w_pw1, conv_b_pw1, conv_w_dw, conv_b_dw,
           conv_ln_g, conv_ln_b, conv_w_pw2, conv_b_pw2, moe_w_router, moe_w_gate, moe_w_up, moe_w_down):
    b, s, d = x.shape
    n_ctx = ctx.shape[1]
    depth = ada_w.shape[0]
    assert depth == 2, "layer schedule below is written for one even and one odd layer"
    na_w = ab_rpb.shape[1] * HEAD_DIM
    sw_q = ab_sinks.shape[1] * HEAD_DIM
    in_w = ab_w_in.shape[2]
    sw_kv = (in_w - 3 * na_w - sw_q) // 2
    n_experts = moe_w_router.shape[2]

    mod_rows = 8
    cvec = jnp.zeros((mod_rows, d), F32).at[:b].set(c).at[b].set(c_ctx)
    mods = ada_mod_all(cvec, ada_w, ada_b).reshape(depth, mod_rows, 1, N_MOD * d)

    x2d = x.reshape(b * s, d)
    ctx2d = ctx.reshape(b * n_ctx, d)

    mod0 = mods[0]
    w_in = ab_w_in[0].astype(BF16)
    tabs = rope_tables(s)
    rope_lo, rope_hi = 3 * na_w, 3 * na_w + sw_q + sw_kv
    tm_in = min(1024, s)
    qkv = in_proj(x2d, norm_g[0, 0], mod0, lambda i: i // (s // tm_in), w_in, tabs, rope_lo, rope_hi, s, tm_in)
    ctx_tabs = tuple(t[:n_ctx] for t in tabs)
    qkv_ctx = in_proj(ctx2d, norm_g[0, 0], mod0, lambda i: b, w_in, ctx_tabs, 0, 0, n_ctx, n_ctx)
    qkv = qkv.reshape(b, s, in_w)
    qkv_ctx = qkv_ctx.reshape(b, n_ctx, in_w)

    na_heads = na_w // HEAD_DIM
    bias = na_bias_table(ab_rpb[0], s // GRID_W, n_ctx)
    y_a = na_attention(qkv, qkv_ctx, bias, na_heads, 0, na_heads, 2 * na_heads)
    y_b = band_attention(qkv, qkv_ctx, ab_sinks[0], sw_q // HEAD_DIM, sw_kv // HEAD_DIM,
                         3 * na_w, 3 * na_w + sw_q, 3 * na_w + sw_q + sw_kv)
    zero_d = jnp.zeros((d,), F32)
    x2d = proj_residual([y_a.reshape(b * s, na_w), y_b.reshape(b * s, sw_q)], ab_w_out[0].astype(BF16),
                        zero_d, x2d, mod0, 2, s)
    x2d = dense_ffn(x2d, norm_g[0, 1], mod0, s, ffn_w_gate[0].astype(BF16), ffn_w_up[0].astype(BF16),
                    ffn_w_down[0].astype(BF16))

    mod1 = mods[1]
    u = pw1_glu(x2d, norm_g[1, 0], mod1, s, conv_w_pw1[0].astype(BF16), conv_b_pw1[0])
    v = conv_ln_swish(u.reshape(b, s, d), conv_w_dw[0], conv_b_dw[0], conv_ln_g[0], conv_ln_b[0])
    x2d = proj_residual([v.reshape(b * s, d)], conv_w_pw2[0].astype(BF16), conv_b_pw2[0], x2d, mod1, 2, s)

    hp, top_i, top_w = router(x2d, norm_g[1, 1], mod1, s, moe_w_router[0])
    tm_e = 1024
    src, row_w, pos, tile_expert, n_tiles = dispatch_plan(top_i, top_w, n_experts, tm_e)
    xp = gather_rows(hp, src)
    y_sorted = expert_ffn(xp, row_w, tile_expert, n_tiles, moe_w_gate[0], moe_w_up[0], moe_w_down[0], tm=tm_e)
    out = combine_final(y_sorted, pos, x2d, mod1, s, final_g)
    return out.reshape(b, s, d)
```

```python
import functools

import numpy as np
import jax
import jax.numpy as jnp
from jax import lax
from jax.experimental import pallas as pl
from jax.experimental.pallas import tpu as pltpu

F32 = jnp.float32
BF16 = jnp.bfloat16

GRID_W = 64
HEAD_DIM = 128
NA_KH = 8
NA_KW = 16
SW_WINDOW = 128
ROPE_BASE = 10000.0
CONV_WIDTH = 31
N_MOD = 6
TOP_K = 2
EPS = 1e-6
NEG = -1e30

VMEM_LIMIT = 56 * 1024 * 1024
NA_QROWS = 8
NA_KROWS = 16
CONV_HALO = 16
FFN_OUT_CHUNK = 512
SUBLANES = 8
DMA_UNROLL = 8


def _pick_tile(n, candidates):
    return next(t for t in candidates if n % t == 0)


def _cparams(sem):
    return pltpu.CompilerParams(dimension_semantics=sem, vmem_limit_bytes=VMEM_LIMIT)


def _silu(v):
    return v / (1.0 + jnp.exp(-v))


def _sigmoid(v):
    return 1.0 / (1.0 + jnp.exp(-v))


def _norm_mod(x, g, sh, sc):
    ms = jnp.mean(x * x, axis=-1, keepdims=True)
    y = x * lax.rsqrt(ms + EPS) * g
    return y * (1.0 + sc) + sh


def _ada_kernel(c_ref, w_ref, b_ref, o_ref):
    s = _silu(c_ref[...]).astype(BF16)
    acc = jnp.dot(s, w_ref[...].astype(BF16), preferred_element_type=F32)
    o_ref[...] = acc + b_ref[...]


def ada_mod_all(cvec, ada_w, ada_b, tn=1024):
    depth, d, n = ada_w.shape
    rows = cvec.shape[0]
    return pl.pallas_call(
        _ada_kernel,
        grid=(depth, n // tn),
        in_specs=[
            pl.BlockSpec((rows, d), lambda l, j: (0, 0)),
            pl.BlockSpec((None, d, tn), lambda l, j: (l, 0, j)),
            pl.BlockSpec((None, 1, tn), lambda l, j: (l, 0, j)),
        ],
        out_specs=pl.BlockSpec((None, rows, tn), lambda l, j: (l, 0, j)),
        out_shape=jax.ShapeDtypeStruct((depth, rows, n), F32),
        compiler_params=_cparams(("parallel", "parallel")),
        name="ada_mod",
    )(cvec, ada_w, ada_b.reshape(depth, 1, n))


def _rope_head(v, cos, sa, sb):
    return v * cos + pltpu.roll(v, 96, 1) * sa + pltpu.roll(v, 32, 1) * sb


def _inproj_kernel(x_ref, g_ref, sh_ref, sc_ref, w_ref, cos_ref, sa_ref, sb_ref, o_ref, h_scr,
                   *, rope_lo, rope_hi, tn):
    j = pl.program_id(1)

    @pl.when(j == 0)
    def _():
        h_scr[...] = _norm_mod(x_ref[...], g_ref[...], sh_ref[...], sc_ref[...]).astype(BF16)

    acc = jnp.dot(h_scr[...], w_ref[...], preferred_element_type=F32)
    heads = tn // HEAD_DIM
    first_tile = rope_lo // tn
    last_tile = (rope_hi - 1) // tn if rope_hi > rope_lo else -1

    def plain():
        o_ref[...] = acc.astype(o_ref.dtype)

    if rope_hi <= rope_lo:
        plain()
        return

    pl.when((j < first_tile) | (j > last_tile))(plain)

    for t in range(first_tile, last_tile + 1):
        def roped(t=t):
            cos, sa, sb = cos_ref[...], sa_ref[...], sb_ref[...]
            for hh in range(heads):
                col = t * tn + hh * HEAD_DIM
                v = acc[:, hh * HEAD_DIM:(hh + 1) * HEAD_DIM]
                if rope_lo <= col < rope_hi:
                    v = _rope_head(v, cos, sa, sb)
                o_ref[:, hh * HEAD_DIM:(hh + 1) * HEAD_DIM] = v.astype(o_ref.dtype)
        pl.when(j == t)(roped)


def in_proj(x2d, g, mod, mod_row_of_tile, w_bf16, rope_tabs, rope_lo, rope_hi, seq, tm):
    r, d = x2d.shape
    n = w_bf16.shape[1]
    tn = _pick_tile(n, (512, 256, 128))
    cos, sa, sb = rope_tabs
    tiles_per_seq = seq // tm
    tab_spec = pl.BlockSpec((tm, HEAD_DIM), lambda i, j: (i % tiles_per_seq, 0))
    kern = functools.partial(_inproj_kernel, rope_lo=rope_lo, rope_hi=rope_hi, tn=tn)
    return pl.pallas_call(
        kern,
        grid=(r // tm, n // tn),
        in_specs=[
            pl.BlockSpec((tm, d), lambda i, j: (i, 0)),
            pl.BlockSpec((1, d), lambda i, j: (0, 0)),
            pl.BlockSpec((None, 1, d), lambda i, j: (mod_row_of_tile(i), 0, 0)),
            pl.BlockSpec((None, 1, d), lambda i, j: (mod_row_of_tile(i), 0, 1)),
            pl.BlockSpec((d, tn), lambda i, j: (0, j)),
            tab_spec, tab_spec, tab_spec,
        ],
        out_specs=pl.BlockSpec((tm, tn), lambda i, j: (i, j)),
        out_shape=jax.ShapeDtypeStruct((r, n), BF16),
        scratch_shapes=[pltpu.VMEM((tm, d), BF16)],
        compiler_params=_cparams(("parallel", "arbitrary")),
        name="in_proj",
    )(x2d, g.reshape(1, d), mod, mod, w_bf16, cos, sa, sb)


def _glu_kernel(x_ref, g_ref, sh_ref, sc_ref, wa_ref, wg_ref, ba_ref, bg_ref, o_ref, h_scr):
    j = pl.program_id(1)

    @pl.when(j == 0)
    def _():
        h_scr[...] = _norm_mod(x_ref[...], g_ref[...], sh_ref[...], sc_ref[...]).astype(BF16)

    h = h_scr[...]
    a = jnp.dot(h, wa_ref[...], preferred_element_type=F32) + ba_ref[...]
    gate = jnp.dot(h, wg_ref[...], preferred_element_type=F32) + bg_ref[...]
    o_ref[...] = a * _sigmoid(gate)


def pw1_glu(x2d, g, mod, seq, w_bf16, b, tm=1024, tn=512):
    r, d = x2d.shape
    half = w_bf16.shape[1] // 2
    nj = half // tn
    tiles_per_seq = seq // tm
    b2 = b.reshape(1, 2 * half)
    return pl.pallas_call(
        _glu_kernel,
        grid=(r // tm, nj),
        in_specs=[
            pl.BlockSpec((tm, d), lambda i, j: (i, 0)),
            pl.BlockSpec((1, d), lambda i, j: (0, 0)),
            pl.BlockSpec((None, 1, d), lambda i, j: (i // tiles_per_seq, 0, 0)),
            pl.BlockSpec((None, 1, d), lambda i, j: (i // tiles_per_seq, 0, 1)),
            pl.BlockSpec((d, tn), lambda i, j: (0, j)),
            pl.BlockSpec((d, tn), lambda i, j: (0, j + nj)),
            pl.BlockSpec((1, tn), lambda i, j: (0, j)),
            pl.BlockSpec((1, tn), lambda i, j: (0, j + nj)),
        ],
        out_specs=pl.BlockSpec((tm, tn), lambda i, j: (i, j)),
        out_shape=jax.ShapeDtypeStruct((r, half), F32),
        scratch_shapes=[pltpu.VMEM((tm, d), BF16)],
        compiler_params=_cparams(("parallel", "arbitrary")),
        name="pw1_glu",
    )(x2d, g.reshape(1, d), mod, mod, w_bf16, w_bf16, b2, b2)


def _proj_res_kernel(*refs, n_in, splits):
    y_refs = refs[:n_in]
    w_ref, b_ref, x_ref, gt_ref, o_ref = refs[n_in:]
    acc = None
    for y_ref, (lo, hi) in zip(y_refs, splits):
        part = jnp.dot(y_ref[...], w_ref[lo:hi, :], preferred_element_type=F32)
        acc = part if acc is None else acc + part
    o_ref[...] = x_ref[...] + gt_ref[...] * (acc + b_ref[...])


def proj_residual(ys, w_bf16, b, x2d, mod, gate_chunk, seq, tm=512):
    r, d = x2d.shape
    kdim, n = w_bf16.shape
    tn = n
    tiles_per_seq = seq // tm
    splits, lo = [], 0
    for y in ys:
        splits.append((lo, lo + y.shape[1]))
        lo += y.shape[1]
    assert lo == kdim
    chunks_per_d = d // tn
    kern = functools.partial(_proj_res_kernel, n_in=len(ys), splits=tuple(splits))
    return pl.pallas_call(
        kern,
        grid=(r // tm, n // tn),
        in_specs=[pl.BlockSpec((tm, y.shape[1]), lambda i, j: (i, 0)) for y in ys] + [
            pl.BlockSpec((kdim, tn), lambda i, j: (0, j)),
            pl.BlockSpec((1, tn), lambda i, j: (0, j)),
            pl.BlockSpec((tm, tn), lambda i, j: (i, j)),
            pl.BlockSpec((None, 1, tn),
                         lambda i, j: (i // tiles_per_seq, 0, gate_chunk * chunks_per_d + j)),
        ],
        out_specs=pl.BlockSpec((tm, tn), lambda i, j: (i, j)),
        out_shape=jax.ShapeDtypeStruct((r, n), F32),
        compiler_params=_cparams(("parallel", "parallel")),
        name="proj_residual",
    )(*ys, w_bf16, b.reshape(1, n), x2d, mod)


def _na_kernel(q_ref, k_ref, v_ref, kc_ref, vc_ref, bias_ref, o_ref, kcat, vcat, *, scale, n_win, n_blocks):
    i = pl.program_id(2)
    first_row = jnp.clip(i * NA_QROWS - (NA_KROWS - NA_QROWS) // 2, 0, (n_blocks * NA_QROWS) - NA_KROWS)
    start = pl.multiple_of(first_row * GRID_W, 256)
    kcat[0:n_win, :] = k_ref[pl.ds(start, n_win), :]
    vcat[0:n_win, :] = v_ref[pl.ds(start, n_win), :]
    kcat[n_win:, :] = kc_ref[...]
    vcat[n_win:, :] = vc_ref[...]
    s = lax.dot_general(q_ref[...], kcat[...], (((1,), (1,)), ((), ())), preferred_element_type=F32)
    s = s * scale + bias_ref[...]
    m = jnp.max(s, axis=-1, keepdims=True)
    p = jnp.exp(s - m)
    l = jnp.sum(p, axis=-1, keepdims=True)
    o = jnp.dot(p.astype(BF16), vcat[...], preferred_element_type=F32)
    o_ref[...] = (o / l).astype(o_ref.dtype)


def na_bias_table(rpb, rows, n_ctx):
    heads = rpb.shape[0]
    c = np.arange(GRID_W)
    dc = c[None, :] - c[:, None]
    col_start = np.clip(c - NA_KW // 2, 0, GRID_W - NA_KW)
    col_valid = (c[None, :] >= col_start[:, None]) & (c[None, :] < col_start[:, None] + NA_KW)
    col_idx = np.clip(dc + NA_KW - 1, 0, 2 * NA_KW - 2)
    toep = jnp.take(rpb, jnp.asarray(col_idx.reshape(-1)), axis=2).reshape(heads, 2 * NA_KH - 1, GRID_W, GRID_W)
    toep = jnp.where(jnp.asarray(col_valid)[None, None], toep, NEG).transpose(0, 2, 1, 3)
    pad = NA_KROWS
    toep = jnp.pad(toep, ((0, 0), (0, 0), (pad, pad), (0, 0)), constant_values=NEG)
    kh = min(NA_KH, rows)
    n_blocks = rows // NA_QROWS
    zeros_ctx = jnp.zeros((heads, GRID_W, n_ctx), F32)
    tables = []
    for q0, k0 in ((0, 0), (NA_QROWS, NA_QROWS - (NA_KROWS - NA_QROWS) // 2),
                   ((n_blocks - 1) * NA_QROWS, rows - NA_KROWS)):
        kr = k0 + np.arange(NA_KROWS)
        pieces = []
        for rq in range(NA_QROWS):
            r = q0 + rq
            rs = min(max(r - kh // 2, 0), rows - kh)
            row_valid = (kr >= rs) & (kr < rs + kh)
            lo = pad + k0 - r + NA_KH - 1
            piece = jnp.where(jnp.asarray(row_valid)[None, None, :, None], toep[:, :, lo:lo + NA_KROWS, :], NEG)
            pieces.append(jnp.concatenate([piece.reshape(heads, GRID_W, NA_KROWS * GRID_W), zeros_ctx], axis=-1))
        tables.append(jnp.concatenate(pieces, axis=1))
    return jnp.stack(tables, axis=0)


def na_attention(qkv, qkv_ctx, bias, heads, q_blk0, k_blk0, v_blk0):
    b, s, _ = qkv.shape
    n_ctx = qkv_ctx.shape[1]
    rows = s // GRID_W
    n_blocks = rows // NA_QROWS
    tq = NA_QROWS * GRID_W
    n_win = NA_KROWS * GRID_W
    nk = n_win + n_ctx

    def variant(i):
        return jnp.where(i == 0, 0, jnp.where(i == n_blocks - 1, 2, 1))

    kern = functools.partial(_na_kernel, scale=HEAD_DIM ** -0.5, n_win=n_win, n_blocks=n_blocks)
    return pl.pallas_call(
        kern,
        grid=(b, heads, n_blocks),
        in_specs=[
            pl.BlockSpec((None, tq, HEAD_DIM), lambda bi, h, i: (bi, i, q_blk0 + h)),
            pl.BlockSpec((None, s, HEAD_DIM), lambda bi, h, i: (bi, 0, k_blk0 + h)),
            pl.BlockSpec((None, s, HEAD_DIM), lambda bi, h, i: (bi, 0, v_blk0 + h)),
            pl.BlockSpec((None, n_ctx, HEAD_DIM), lambda bi, h, i: (bi, 0, k_blk0 + h)),
            pl.BlockSpec((None, n_ctx, HEAD_DIM), lambda bi, h, i: (bi, 0, v_blk0 + h)),
            pl.BlockSpec((None, None, tq, nk), lambda bi, h, i: (variant(i), h, 0, 0)),
        ],
        out_specs=pl.BlockSpec((None, tq, HEAD_DIM), lambda bi, h, i: (bi, i, h)),
        out_shape=jax.ShapeDtypeStruct((b, s, heads * HEAD_DIM), BF16),
        scratch_shapes=[pltpu.VMEM((nk, HEAD_DIM), BF16), pltpu.VMEM((nk, HEAD_DIM), BF16)],
        compiler_params=_cparams(("parallel", "parallel", "arbitrary")),
        name="na_attention",
    )(qkv, qkv, qkv, qkv_ctx, qkv_ctx, bias)


def _band_kernel(q_ref, k_ref, v_ref, kc_ref, vc_ref, sink_ref, o_ref, kcat, vcat, *, scale, blk, group, seq):
    kvh = pl.program_id(1)
    n = pl.program_id(2)
    n_win = 3 * blk
    start = pl.multiple_of(jnp.clip((n - 1) * blk, 0, seq - n_win), blk)
    kcat[0:n_win, :] = k_ref[pl.ds(start, n_win), :]
    vcat[0:n_win, :] = v_ref[pl.ds(start, n_win), :]
    kcat[n_win:, :] = kc_ref[...]
    vcat[n_win:, :] = vc_ref[...]
    q = jnp.concatenate([q_ref[:, g * HEAD_DIM:(g + 1) * HEAD_DIM] for g in range(group)], axis=0)
    s = lax.dot_general(q, kcat[...], (((1,), (1,)), ((), ())), preferred_element_type=F32) * scale
    nk = kcat.shape[0]
    rows = group * blk
    qpos = n * blk + lax.broadcasted_iota(jnp.int32, (rows, nk), 0) % blk
    col = lax.broadcasted_iota(jnp.int32, (rows, nk), 1)
    kpos = start + col
    valid = (col >= n_win) | (jnp.abs(qpos - kpos) <= SW_WINDOW)
    s = jnp.where(valid, s, NEG)
    head = lax.broadcasted_iota(jnp.int32, (rows, 1), 0) // blk
    sink = jnp.zeros((rows, 1), F32)
    for g in range(group):
        sink = jnp.where(head == g, sink_ref[kvh * group + g], sink)
    m = jnp.maximum(jnp.max(s, axis=-1, keepdims=True), sink)
    p = jnp.exp(s - m)
    l = jnp.sum(p, axis=-1, keepdims=True) + jnp.exp(sink - m)
    o = jnp.dot(p.astype(BF16), vcat[...], preferred_element_type=F32) / l
    for g in range(group):
        o_ref[:, g * HEAD_DIM:(g + 1) * HEAD_DIM] = o[g * blk:(g + 1) * blk, :].astype(o_ref.dtype)


def band_attention(qkv, qkv_ctx, sinks, q_heads, kv_heads, q_col0, k_col0, v_col0, blk=128):
    b, s, _ = qkv.shape
    n_ctx = qkv_ctx.shape[1]
    group = q_heads // kv_heads
    gw = group * HEAD_DIM
    nk = 3 * blk + n_ctx
    kern = functools.partial(_band_kernel, scale=HEAD_DIM ** -0.5, blk=blk, group=group, seq=s)
    return pl.pallas_call(
        kern,
        grid=(b, kv_heads, s // blk),
        in_specs=[
            pl.BlockSpec((None, blk, gw), lambda bi, kv, n: (bi, n, q_col0 // gw + kv)),
            pl.BlockSpec((None, s, HEAD_DIM), lambda bi, kv, n: (bi, 0, k_col0 // HEAD_DIM + kv)),
            pl.BlockSpec((None, s, HEAD_DIM), lambda bi, kv, n: (bi, 0, v_col0 // HEAD_DIM + kv)),
            pl.BlockSpec((None, n_ctx, HEAD_DIM), lambda bi, kv, n: (bi, 0, k_col0 // HEAD_DIM + kv)),
            pl.BlockSpec((None, n_ctx, HEAD_DIM), lambda bi, kv, n: (bi, 0, v_col0 // HEAD_DIM + kv)),
            pl.BlockSpec(memory_space=pltpu.SMEM),
        ],
        out_specs=pl.BlockSpec((None, blk, gw), lambda bi, kv, n: (bi, n, kv)),
        out_shape=jax.ShapeDtypeStruct((b, s, q_heads * HEAD_DIM), BF16),
        scratch_shapes=[pltpu.VMEM((nk, HEAD_DIM), BF16), pltpu.VMEM((nk, HEAD_DIM), BF16)],
        compiler_params=_cparams(("parallel", "parallel", "arbitrary")),
        name="band_attention",
    )(qkv, qkv, qkv, qkv_ctx, qkv_ctx, sinks)


def _ffn_kernel(x_ref, g_ref, sh_ref, sc_ref, gt_ref, wg_ref, wu_ref, wd_ref, o_ref, h_scr):
    f = pl.program_id(1)

    @pl.when(f == 0)
    def _():
        h_scr[...] = _norm_mod(x_ref[...], g_ref[...], sh_ref[...], sc_ref[...]).astype(BF16)
        o_ref[...] = jnp.zeros_like(o_ref)

    h = h_scr[...]
    gate = jnp.dot(h, wg_ref[...], preferred_element_type=F32)
    up = jnp.dot(h, wu_ref[...], preferred_element_type=F32)
    a = (_silu(gate) * up).astype(BF16)
    for c0 in range(0, o_ref.shape[1], FFN_OUT_CHUNK):
        cs = slice(c0, c0 + FFN_OUT_CHUNK)
        o_ref[:, cs] += jnp.dot(a, wd_ref[:, cs], preferred_element_type=F32)

    @pl.when(f == pl.num_programs(1) - 1)
    def _():
        o_ref[...] = x_ref[...] + gt_ref[...] * o_ref[...]


def dense_ffn(x2d, g, mod, seq, wg, wu, wd, tm=1024, tf=512):
    r, d = x2d.shape
    dff = wg.shape[1]
    tiles_per_seq = seq // tm
    row = lambda i, f: i // tiles_per_seq
    return pl.pallas_call(
        _ffn_kernel,
        grid=(r // tm, dff // tf),
        in_specs=[
            pl.BlockSpec((tm, d), lambda i, f: (i, 0), pipeline_mode=pl.Buffered(1)),
            pl.BlockSpec((1, d), lambda i, f: (0, 0)),
            pl.BlockSpec((None, 1, d), lambda i, f: (row(i, f), 0, 3)),
            pl.BlockSpec((None, 1, d), lambda i, f: (row(i, f), 0, 4)),
            pl.BlockSpec((None, 1, d), lambda i, f: (row(i, f), 0, 5)),
            pl.BlockSpec((d, tf), lambda i, f: (0, f)),
            pl.BlockSpec((d, tf), lambda i, f: (0, f)),
            pl.BlockSpec((tf, d), lambda i, f: (f, 0)),
        ],
        out_specs=pl.BlockSpec((tm, d), lambda i, f: (i, 0)),
        out_shape=jax.ShapeDtypeStruct((r, d), F32),
        scratch_shapes=[pltpu.VMEM((tm, d), BF16)],
        compiler_params=_cparams(("parallel", "arbitrary")),
        name="dense_ffn",
    )(x2d, g.reshape(1, d), mod, mod, mod, wg, wu, wd)


def _conv_kernel(prev_ref, cur_ref, next_ref, w_ref, b_ref, g_ref, beta_ref, o_ref, ubuf, cbuf, *, ts, rb, cb):
    i = pl.program_id(1)
    last = pl.num_programs(1) - 1
    d = cur_ref.shape[1]
    zeros = jnp.zeros((CONV_HALO, d), F32)

    @pl.when(i == 0)
    def _():
        ubuf[0, 0:CONV_HALO, :] = zeros

    @pl.when(i > 0)
    def _():
        ubuf[0, 0:CONV_HALO, :] = prev_ref[...]

    ubuf[0, CONV_HALO:CONV_HALO + ts, :] = cur_ref[...]

    @pl.when(i == last)
    def _():
        ubuf[0, CONV_HALO + ts:, :] = zeros

    @pl.when(i < last)
    def _():
        ubuf[0, CONV_HALO + ts:, :] = next_ref[...]

    off = CONV_HALO - CONV_WIDTH // 2
    rows_read = ts + (off + CONV_WIDTH - 1) // SUBLANES * SUBLANES
    for s in range(1, SUBLANES):
        for a in range(0, rows_read, SUBLANES):
            ubuf[s, a:a + SUBLANES, :] = ubuf[0, a + s:a + s + SUBLANES, :]

    for c0 in range(0, d, cb):
        wt = w_ref[:, c0:c0 + cb]
        bias = b_ref[:, c0:c0 + cb]

        def row_body(rt, carry, c0=c0, wt=wt, bias=bias):
            r0 = pl.multiple_of(rt * rb, rb)
            acc = jnp.zeros((rb, cb), F32) + bias
            for k in range(CONV_WIDTH):
                s, q = (off + k) % SUBLANES, (off + k) // SUBLANES
                acc = acc + ubuf[s, pl.ds(r0 + q * SUBLANES, rb), c0:c0 + cb] * wt[k:k + 1, :]
            cbuf[pl.ds(r0, rb), c0:c0 + cb] = acc
            return carry

        lax.fori_loop(0, ts // rb, row_body, 0)

    v = cbuf[...]
    mu = jnp.mean(v, axis=-1, keepdims=True)
    cen = v - mu
    var = jnp.mean(cen * cen, axis=-1, keepdims=True)
    y = cen * lax.rsqrt(var + EPS) * g_ref[...] + beta_ref[...]
    o_ref[...] = _silu(y).astype(o_ref.dtype)


def conv_ln_swish(u, w_dw, b_dw, ln_g, ln_b, ts=256, rb=32, cb=256):
    b, s, d = u.shape
    hb = ts // CONV_HALO
    n_halo = s // CONV_HALO
    kern = functools.partial(_conv_kernel, ts=ts, rb=rb, cb=cb)
    return pl.pallas_call(
        kern,
        grid=(b, s // ts),
        in_specs=[
            pl.BlockSpec((None, CONV_HALO, d), lambda bi, i: (bi, jnp.maximum(i * hb - 1, 0), 0)),
            pl.BlockSpec((None, ts, d), lambda bi, i: (bi, i, 0)),
            pl.BlockSpec((None, CONV_HALO, d), lambda bi, i: (bi, jnp.minimum((i + 1) * hb, n_halo - 1), 0)),
            pl.BlockSpec((CONV_WIDTH, d), lambda bi, i: (0, 0)),
            pl.BlockSpec((1, d), lambda bi, i: (0, 0)),
            pl.BlockSpec((1, d), lambda bi, i: (0, 0)),
            pl.BlockSpec((1, d), lambda bi, i: (0, 0)),
        ],
        out_specs=pl.BlockSpec((None, ts, d), lambda bi, i: (bi, i, 0)),
        out_shape=jax.ShapeDtypeStruct((b, s, d), BF16),
        scratch_shapes=[pltpu.VMEM((SUBLANES, ts + 2 * CONV_HALO, d), F32), pltpu.VMEM((ts, d), F32)],
        compiler_params=_cparams(("parallel", "arbitrary")),
        name="conv_ln_swish",
    )(u, u, u, w_dw, b_dw.reshape(1, d), ln_g.reshape(1, d), ln_b.reshape(1, d))


def _bf16_bits(v):
    b = pltpu.bitcast(v, jnp.uint32)
    return (b + jnp.uint32(0x7FFF) + ((b >> 16) & jnp.uint32(1))) >> 16


def _router_kernel(x_ref, g_ref, sh_ref, sc_ref, wh_ref, wl_ref, hp_ref, idx_ref, wt_ref, *, n_experts):
    h = _norm_mod(x_ref[...], g_ref[...], sh_ref[...], sc_ref[...])
    half = h.shape[1] // 2
    hp_ref[...] = (_bf16_bits(h[:, half:]) << 16) | _bf16_bits(h[:, :half])
    h_hi = h.astype(BF16)
    h_lo = (h - h_hi.astype(F32)).astype(BF16)
    wh, wl = wh_ref[...], wl_ref[...]
    logits = (jnp.dot(h_hi, wh, preferred_element_type=F32)
              + (jnp.dot(h_hi, wl, preferred_element_type=F32) + jnp.dot(h_lo, wh, preferred_element_type=F32)))
    lane = lax.broadcasted_iota(jnp.int32, logits.shape, 1)
    big = logits.shape[1]
    logits = jnp.where(lane < n_experts, logits, -jnp.inf)
    m1 = jnp.max(logits, axis=-1, keepdims=True)
    i1 = jnp.min(jnp.where(logits == m1, lane, big), axis=-1, keepdims=True)
    rest = jnp.where(lane == i1, -jnp.inf, logits)
    m2 = jnp.max(rest, axis=-1, keepdims=True)
    i2 = jnp.min(jnp.where(rest == m2, lane, big), axis=-1, keepdims=True)
    e = jnp.exp(m2 - m1)
    w1 = 1.0 / (1.0 + e)
    w2 = e / (1.0 + e)
    idx_ref[...] = jnp.where(lane == 0, i1, jnp.where(lane == 1, i2, 0))
    wt_ref[...] = jnp.where(lane == 0, w1, jnp.where(lane == 1, w2, 0.0))


def router(x2d, g, mod, seq, w_router, tm=512):
    r, d = x2d.shape
    n_experts = w_router.shape[1]
    lanes = 128
    wpad = jnp.zeros((d, lanes), F32).at[:, :n_experts].set(w_router)
    wh = wpad.astype(BF16)
    wl = (wpad - wh.astype(F32)).astype(BF16)
    tiles_per_seq = seq // tm
    kern = functools.partial(_router_kernel, n_experts=n_experts)
    hp, idx, wt = pl.pallas_call(
        kern,
        grid=(r // tm,),
        in_specs=[
            pl.BlockSpec((tm, d), lambda i: (i, 0)),
            pl.BlockSpec((1, d), lambda i: (0, 0)),
            pl.BlockSpec((None, 1, d), lambda i: (i // tiles_per_seq, 0, 3)),
            pl.BlockSpec((None, 1, d), lambda i: (i // tiles_per_seq, 0, 4)),
            pl.BlockSpec((d, lanes), lambda i: (0, 0)),
            pl.BlockSpec((d, lanes), lambda i: (0, 0)),
        ],
        out_specs=[
            pl.BlockSpec((tm, d // 2), lambda i: (i, 0)),
            pl.BlockSpec((tm, lanes), lambda i: (i, 0)),
            pl.BlockSpec((tm, lanes), lambda i: (i, 0)),
        ],
        out_shape=[
            jax.ShapeDtypeStruct((r, d // 2), jnp.uint32),
            jax.ShapeDtypeStruct((r, lanes), jnp.int32),
            jax.ShapeDtypeStruct((r, lanes), F32),
        ],
        compiler_params=_cparams(("parallel",)),
        name="router",
    )(x2d, g.reshape(1, d), mod, mod, wh, wl)
    return hp, idx[:, :TOP_K], wt[:, :TOP_K]


def _issue_rows(table_ref, idx_ref, idx_base, dst_of_row, sem, row0, n_rows):
    def body(g, carry):
        for u in range(DMA_UNROLL):
            r = row0 + g * DMA_UNROLL + u
            pltpu.make_async_copy(table_ref.at[pl.ds(idx_ref[idx_base + r], 1)], dst_of_row(r), sem).start()
        return carry

    lax.fori_loop(0, n_rows // DMA_UNROLL, body, 0)


def _expert_kernel(te_ref, nt_ref, src_ref, hp_ref, wg_ref, wu_ref, wd_ref, o_ref, xbuf, x_scr, sem,
                   *, tm, rows_per_step):
    i = pl.program_id(0)
    f = pl.program_id(1)
    n_live = nt_ref[0]
    live = i < n_live
    slot = i % 2

    @pl.when(jnp.logical_not(live) & (f == 0))
    def _():
        o_ref[...] = jnp.zeros_like(o_ref)

    @pl.when((i == 0) & (f == 0))
    def _():
        _issue_rows(hp_ref, src_ref, 0, lambda r: xbuf.at[0, pl.ds(r, 1)], sem.at[0], 0, tm)

    @pl.when((i + 1 < n_live) & (f < tm // rows_per_step))
    def _():
        _issue_rows(hp_ref, src_ref, (i + 1) * tm, lambda r: xbuf.at[1 - slot, pl.ds(r, 1)], sem.at[1 - slot],
                    f * rows_per_step, rows_per_step)

    @pl.when(live & (f == 0))
    def _():
        pltpu.make_async_copy(hp_ref.at[pl.ds(0, tm)], xbuf.at[slot], sem.at[slot]).wait()
        xp = xbuf[slot]
        lo = pltpu.bitcast(xp << 16, F32)
        hi = pltpu.bitcast(xp & jnp.uint32(0xFFFF0000), F32)
        half = xp.shape[1]
        x_scr[:, :half] = lo.astype(BF16)
        x_scr[:, half:] = hi.astype(BF16)
        o_ref[...] = jnp.zeros_like(o_ref)

    @pl.when(live)
    def _():
        x = x_scr[...]
        gate = jnp.dot(x, wg_ref[...].astype(BF16), preferred_element_type=F32)
        up = jnp.dot(x, wu_ref[...].astype(BF16), preferred_element_type=F32)
        a = (_silu(gate) * up).astype(BF16)
        for c0 in range(0, o_ref.shape[1], FFN_OUT_CHUNK):
            cs = slice(c0, c0 + FFN_OUT_CHUNK)
            o_ref[:, cs] += jnp.dot(a, wd_ref[:, cs].astype(BF16), preferred_element_type=F32)


def expert_ffn(hp, src, tile_expert, n_tiles, wg, wu, wd, tm=1024, tf=256):
    half = hp.shape[1]
    r = src.shape[0]
    d = 2 * half
    dff = wg.shape[2]
    nf = dff // tf
    fetch_steps = next(n for n in range(min(nf, tm // DMA_UNROLL), 0, -1) if (tm // DMA_UNROLL) % n == 0)
    rows_per_step = tm // fetch_steps

    def live_tile(i, nt):
        return jnp.minimum(i, nt[0] - 1)

    def wf(i, f, nt):
        return jnp.where(i < nt[0], f, nf - 1)

    kern = functools.partial(_expert_kernel, tm=tm, rows_per_step=rows_per_step)
    return pl.pallas_call(
        kern,
        grid_spec=pltpu.PrefetchScalarGridSpec(
            num_scalar_prefetch=3,
            grid=(r // tm, nf),
            in_specs=[
                pl.BlockSpec(memory_space=pl.ANY),
                pl.BlockSpec((None, d, tf), lambda i, f, te, nt, src: (te[live_tile(i, nt)], 0, wf(i, f, nt))),
                pl.BlockSpec((None, d, tf), lambda i, f, te, nt, src: (te[live_tile(i, nt)], 0, wf(i, f, nt))),
                pl.BlockSpec((None, tf, d), lambda i, f, te, nt, src: (te[live_tile(i, nt)], wf(i, f, nt), 0)),
            ],
            out_specs=pl.BlockSpec((tm, d), lambda i, f, te, nt, src: (i, 0)),
            scratch_shapes=[pltpu.VMEM((2, tm, half), jnp.uint32), pltpu.VMEM((tm, d), BF16),
                            pltpu.SemaphoreType.DMA((2,))],
        ),
        out_shape=jax.ShapeDtypeStruct((r, d), F32),
        compiler_params=_cparams(("arbitrary", "arbitrary")),
        name="expert_ffn",
    )(tile_expert, n_tiles, src, hp, wg, wu, wd)


def _combine_kernel(pos_ref, y_ref, x_ref, w_ref, gt_ref, fg_ref, o_ref, buf, sem, *, tt):
    i = pl.program_id(0)
    slot = i % 2

    def fetch(tile, dst_slot):
        for k in range(TOP_K):
            _issue_rows(y_ref, pos_ref, tile * tt * TOP_K + k * tt, lambda r, k=k: buf.at[dst_slot, k, pl.ds(r, 1)],
                        sem.at[dst_slot], 0, tt)

    @pl.when(i == 0)
    def _():
        fetch(0, 0)

    @pl.when(i + 1 < pl.num_programs(0))
    def _():
        fetch(i + 1, 1 - slot)

    for k in range(TOP_K):
        pltpu.make_async_copy(y_ref.at[pl.ds(0, tt)], buf.at[slot, k], sem.at[slot]).wait()
    w = w_ref[...]
    y = buf[slot, 0] * w[:, 0:1]
    for k in range(1, TOP_K):
        y = y + buf[slot, k] * w[:, k:k + 1]
    x = x_ref[...] + gt_ref[...] * y
    ms = jnp.mean(x * x, axis=-1, keepdims=True)
    o_ref[...] = x * lax.rsqrt(ms + EPS) * fg_ref[...]


def combine_final(y_sorted, pos_kmajor, top_w, x2d, mod, seq, final_g, tt=256):
    r, d = x2d.shape
    tiles_per_seq = seq // tt
    kern = functools.partial(_combine_kernel, tt=tt)
    return pl.pallas_call(
        kern,
        grid_spec=pltpu.PrefetchScalarGridSpec(
            num_scalar_prefetch=1,
            grid=(r // tt,),
            in_specs=[
                pl.BlockSpec(memory_space=pl.ANY),
                pl.BlockSpec((tt, d), lambda i, pos: (i, 0)),
                pl.BlockSpec((tt, TOP_K), lambda i, pos: (i, 0)),
                pl.BlockSpec((None, 1, d), lambda i, pos: (i // tiles_per_seq, 0, 5)),
                pl.BlockSpec((1, d), lambda i, pos: (0, 0)),
            ],
            out_specs=pl.BlockSpec((tt, d), lambda i, pos: (i, 0)),
            scratch_shapes=[pltpu.VMEM((2, TOP_K, tt, d), F32), pltpu.SemaphoreType.DMA((2,))],
        ),
        out_shape=jax.ShapeDtypeStruct((r, d), F32),
        compiler_params=_cparams(("arbitrary",)),
        name="combine_final",
    )(pos_kmajor, y_sorted, x2d, top_w, mod, final_g.reshape(1, d))


def dispatch_plan(top_i, n_experts, tm, tt):
    t = top_i.shape[0]
    flat_e = top_i.reshape(-1)
    onehot = (flat_e[:, None] == jnp.arange(n_experts)[None, :]).astype(jnp.int32)
    rank = jnp.cumsum(onehot, axis=0) - onehot
    counts = jnp.sum(onehot, axis=0)
    tiles = (counts + tm - 1) // tm
    tile_end = jnp.cumsum(tiles)
    group_start = (tile_end - tiles) * tm
    pos = jnp.sum(onehot * (group_start[None, :] + rank), axis=1)
    n_rows = (t * TOP_K // tm + n_experts) * tm
    src = jnp.zeros((n_rows,), jnp.int32).at[pos].set(jnp.arange(t * TOP_K, dtype=jnp.int32) // TOP_K)
    tile_ids = jnp.arange(n_rows // tm)
    tile_expert = jnp.minimum(jnp.sum(tile_ids[:, None] >= tile_end[None, :], axis=1), n_experts - 1)
    pos_kmajor = pos.astype(jnp.int32).reshape(t // tt, tt, TOP_K).transpose(0, 2, 1).reshape(-1)
    return src, pos_kmajor, tile_expert.astype(jnp.int32), tile_end[-1:].astype(jnp.int32)


def rope_tables(seq):
    t = np.arange(seq)
    row, col = t // GRID_W, t % GRID_W
    quarter = HEAD_DIM // 4
    inv = ROPE_BASE ** (-np.arange(0, 2 * quarter, 2, dtype=np.float32) / (2 * quarter))
    lane = np.arange(HEAD_DIM)
    pos = np.where(lane[None, :] < HEAD_DIM // 2, row[:, None], col[:, None]).astype(np.float32)
    ang = jnp.asarray(pos) * jnp.asarray(inv[lane % quarter])[None, :]
    cos, sin = jnp.cos(ang), jnp.sin(ang)
    first = jnp.asarray(((lane // quarter) % 2 == 0)[None, :])
    return cos, jnp.where(first, -sin, 0.0), jnp.where(first, 0.0, sin)


def kernel(x, c, ctx, c_ctx, ada_w, ada_b, norm_g, final_g, ab_w_in, ab_rpb, ab_sinks, ab_w_out,
           ffn_w_gate, ffn_w_up, ffn_w_down, conv_w_pw1, conv_b_pw1, conv_w_dw, conv_b_dw,
           conv_ln_g, conv_ln_b, conv_w_pw2, conv_b_pw2, moe_w_router, moe_w_gate, moe_w_up, moe_w_down):
    b, s, d = x.shape
    n_ctx = ctx.shape[1]
    depth = ada_w.shape[0]
    assert depth == 2, "layer schedule below is written for one even and one odd layer"
    na_w = ab_rpb.shape[1] * HEAD_DIM
    sw_q = ab_sinks.shape[1] * HEAD_DIM
    in_w = ab_w_in.shape[2]
    sw_kv = (in_w - 3 * na_w - sw_q) // 2
    n_experts = moe_w_router.shape[2]

    mod_rows = 8
    cvec = jnp.zeros((mod_rows, d), F32).at[:b].set(c).at[b].set(c_ctx)
    mods = ada_mod_all(cvec, ada_w, ada_b).reshape(depth, mod_rows, 1, N_MOD * d)

    x2d = x.reshape(b * s, d)
    ctx2d = ctx.reshape(b * n_ctx, d)

    mod0 = mods[0]
    w_in = ab_w_in[0].astype(BF16)
    tabs = rope_tables(s)
    rope_lo, rope_hi = 3 * na_w, 3 * na_w + sw_q + sw_kv
    tm_in = min(1024, s)
    qkv = in_proj(x2d, norm_g[0, 0], mod0, lambda i: i // (s // tm_in), w_in, tabs, rope_lo, rope_hi, s, tm_in)
    ctx_tabs = tuple(t[:n_ctx] for t in tabs)
    qkv_ctx = in_proj(ctx2d, norm_g[0, 0], mod0, lambda i: b, w_in, ctx_tabs, 0, 0, n_ctx, n_ctx)
    qkv = qkv.reshape(b, s, in_w)
    qkv_ctx = qkv_ctx.reshape(b, n_ctx, in_w)

    na_heads = na_w // HEAD_DIM
    bias = na_bias_table(ab_rpb[0], s // GRID_W, n_ctx)
    y_a = na_attention(qkv, qkv_ctx, bias, na_heads, 0, na_heads, 2 * na_heads)
    y_b = band_attention(qkv, qkv_ctx, ab_sinks[0], sw_q // HEAD_DIM, sw_kv // HEAD_DIM,
                         3 * na_w, 3 * na_w + sw_q, 3 * na_w + sw_q + sw_kv)
    zero_d = jnp.zeros((d,), F32)
    x2d = proj_residual([y_a.reshape(b * s, na_w), y_b.reshape(b * s, sw_q)], ab_w_out[0].astype(BF16),
                        zero_d, x2d, mod0, 2, s)
    x2d = dense_ffn(x2d, norm_g[0, 1], mod0, s, ffn_w_gate[0].astype(BF16), ffn_w_up[0].astype(BF16),
                    ffn_w_down[0].astype(BF16))

    mod1 = mods[1]
    u = pw1_glu(x2d, norm_g[1, 0], mod1, s, conv_w_pw1[0].astype(BF16), conv_b_pw1[0])
    v = conv_ln_swish(u.reshape(b, s, d), conv_w_dw[0], conv_b_dw[0], conv_ln_g[0], conv_ln_b[0])
    x2d = proj_residual([v.reshape(b * s, d)], conv_w_pw2[0].astype(BF16), conv_b_pw2[0], x2d, mod1, 2, s)

    hp, top_i, top_w = router(x2d, norm_g[1, 1], mod1, s, moe_w_router[0])
    tm_e, tt = 1024, 256
    src, pos_kmajor, tile_expert, n_tiles = dispatch_plan(top_i, n_experts, tm_e, tt)
    y_sorted = expert_ffn(hp, src, tile_expert, n_tiles, moe_w_gate[0], moe_w_up[0], moe_w_down[0], tm=tm_e)
    out = combine_final(y_sorted, pos_kmajor, top_w, x2d, mod1, s, final_g, tt=tt)
    return out.reshape(b, s, d)
```

```python
import functools

import numpy as np
import jax
import jax.numpy as jnp
from jax import lax
from jax.experimental import pallas as pl
from jax.experimental.pallas import tpu as pltpu

F32 = jnp.float32
BF16 = jnp.bfloat16

GRID_W = 64
HEAD_DIM = 128
NA_KH = 8
NA_KW = 16
SW_WINDOW = 128
ROPE_BASE = 10000.0
CONV_WIDTH = 31
N_MOD = 6
TOP_K = 2
EPS = 1e-6
NEG = -1e30

VMEM_LIMIT = 56 * 1024 * 1024
NA_QROWS = 8
CONV_HALO = 16
CONV_COPY_ROWS, CONV_COPY_COLS = 64, 512
FFN_OUT_CHUNK = 512
SUBLANES = 8
EXPERT_ROW_BLOCK = 256
EXPERT_SPARSE_BLOCKS = 2
DMA_UNROLL = 8
ATTN_LOOKAHEAD = 3


def _pick_tile(n, candidates):
    return next(t for t in candidates if n % t == 0)


def _cparams(sem):
    return pltpu.CompilerParams(dimension_semantics=sem, vmem_limit_bytes=VMEM_LIMIT)


def _silu(v):
    return v / (1.0 + jnp.exp(-v))


def _sigmoid(v):
    return 1.0 / (1.0 + jnp.exp(-v))


def _norm_mod(x, g, sh, sc):
    ms = jnp.mean(x * x, axis=-1, keepdims=True)
    y = x * lax.rsqrt(ms + EPS) * g
    return y * (1.0 + sc) + sh


def _ada_kernel(c_ref, w_ref, b_ref, o_ref):
    s = _silu(c_ref[...]).astype(BF16)
    acc = jnp.dot(s, w_ref[...].astype(BF16), preferred_element_type=F32)
    o_ref[...] = acc + b_ref[...]


def ada_mod_all(cvec, ada_w, ada_b, tn=1024):
    depth, d, n = ada_w.shape
    rows = cvec.shape[0]
    return pl.pallas_call(
        _ada_kernel,
        grid=(depth, n // tn),
        in_specs=[
            pl.BlockSpec((rows, d), lambda l, j: (0, 0)),
            pl.BlockSpec((None, d, tn), lambda l, j: (l, 0, j)),
            pl.BlockSpec((None, 1, tn), lambda l, j: (l, 0, j)),
        ],
        out_specs=pl.BlockSpec((None, rows, tn), lambda l, j: (l, 0, j)),
        out_shape=jax.ShapeDtypeStruct((depth, rows, n), F32),
        compiler_params=_cparams(("parallel", "parallel")),
        name="ada_mod",
    )(cvec, ada_w, ada_b.reshape(depth, 1, n))


def _rope_head(v, cos, sa, sb):
    return v * cos + pltpu.roll(v, 96, 1) * sa + pltpu.roll(v, 32, 1) * sb


def _inproj_kernel(x_ref, g_ref, sh_ref, sc_ref, w_ref, cos_ref, sa_ref, sb_ref, o_ref, h_scr,
                   *, rope_lo, rope_hi, tn):
    j = pl.program_id(1)

    @pl.when(j == 0)
    def _():
        h_scr[...] = _norm_mod(x_ref[...], g_ref[...], sh_ref[...], sc_ref[...]).astype(BF16)

    acc = jnp.dot(h_scr[...], w_ref[...], preferred_element_type=F32)
    heads = tn // HEAD_DIM
    first_tile = rope_lo // tn
    last_tile = (rope_hi - 1) // tn if rope_hi > rope_lo else -1

    def plain():
        o_ref[...] = acc.astype(o_ref.dtype)

    if rope_hi <= rope_lo:
        plain()
        return

    pl.when((j < first_tile) | (j > last_tile))(plain)

    for t in range(first_tile, last_tile + 1):
        def roped(t=t):
            cos, sa, sb = cos_ref[...], sa_ref[...], sb_ref[...]
            for hh in range(heads):
                col = t * tn + hh * HEAD_DIM
                v = acc[:, hh * HEAD_DIM:(hh + 1) * HEAD_DIM]
                if rope_lo <= col < rope_hi:
                    v = _rope_head(v, cos, sa, sb)
                o_ref[:, hh * HEAD_DIM:(hh + 1) * HEAD_DIM] = v.astype(o_ref.dtype)
        pl.when(j == t)(roped)


def in_proj(x2d, g, mod, mod_row_of_tile, w_bf16, rope_tabs, rope_lo, rope_hi, seq, tm):
    r, d = x2d.shape
    n = w_bf16.shape[1]
    tn = _pick_tile(n, (512, 256, 128))
    cos, sa, sb = rope_tabs
    tiles_per_seq = seq // tm
    tab_spec = pl.BlockSpec((tm, HEAD_DIM), lambda i, j: (i % tiles_per_seq, 0))
    kern = functools.partial(_inproj_kernel, rope_lo=rope_lo, rope_hi=rope_hi, tn=tn)
    return pl.pallas_call(
        kern,
        grid=(r // tm, n // tn),
        in_specs=[
            pl.BlockSpec((tm, d), lambda i, j: (i, 0)),
            pl.BlockSpec((1, d), lambda i, j: (0, 0)),
            pl.BlockSpec((None, 1, d), lambda i, j: (mod_row_of_tile(i), 0, 0)),
            pl.BlockSpec((None, 1, d), lambda i, j: (mod_row_of_tile(i), 0, 1)),
            pl.BlockSpec((d, tn), lambda i, j: (0, j)),
            tab_spec, tab_spec, tab_spec,
        ],
        out_specs=pl.BlockSpec((tm, tn), lambda i, j: (i, j)),
        out_shape=jax.ShapeDtypeStruct((r, n), BF16),
        scratch_shapes=[pltpu.VMEM((tm, d), BF16)],
        compiler_params=_cparams(("parallel", "arbitrary")),
        name="in_proj",
    )(x2d, g.reshape(1, d), mod, mod, w_bf16, cos, sa, sb)


def _glu_kernel(x_ref, g_ref, sh_ref, sc_ref, wa_ref, wg_ref, ba_ref, bg_ref, o_ref, h_scr):
    j = pl.program_id(1)

    @pl.when(j == 0)
    def _():
        h_scr[...] = _norm_mod(x_ref[...], g_ref[...], sh_ref[...], sc_ref[...]).astype(BF16)

    h = h_scr[...]
    a = jnp.dot(h, wa_ref[...], preferred_element_type=F32) + ba_ref[...]
    gate = jnp.dot(h, wg_ref[...], preferred_element_type=F32) + bg_ref[...]
    o_ref[...] = a * _sigmoid(gate)


def pw1_glu(x2d, g, mod, seq, w_bf16, b, tm=1024, tn=512):
    r, d = x2d.shape
    half = w_bf16.shape[1] // 2
    nj = half // tn
    tiles_per_seq = seq // tm
    b2 = b.reshape(1, 2 * half)
    return pl.pallas_call(
        _glu_kernel,
        grid=(r // tm, nj),
        in_specs=[
            pl.BlockSpec((tm, d), lambda i, j: (i, 0)),
            pl.BlockSpec((1, d), lambda i, j: (0, 0)),
            pl.BlockSpec((None, 1, d), lambda i, j: (i // tiles_per_seq, 0, 0)),
            pl.BlockSpec((None, 1, d), lambda i, j: (i // tiles_per_seq, 0, 1)),
            pl.BlockSpec((d, tn), lambda i, j: (0, j)),
            pl.BlockSpec((d, tn), lambda i, j: (0, j + nj)),
            pl.BlockSpec((1, tn), lambda i, j: (0, j)),
            pl.BlockSpec((1, tn), lambda i, j: (0, j + nj)),
        ],
        out_specs=pl.BlockSpec((tm, tn), lambda i, j: (i, j)),
        out_shape=jax.ShapeDtypeStruct((r, half), F32),
        scratch_shapes=[pltpu.VMEM((tm, d), BF16)],
        compiler_params=_cparams(("parallel", "arbitrary")),
        name="pw1_glu",
    )(x2d, g.reshape(1, d), mod, mod, w_bf16, w_bf16, b2, b2)


def _proj_res_kernel(*refs, n_in, splits):
    y_refs = refs[:n_in]
    w_ref, b_ref, x_ref, gt_ref, o_ref = refs[n_in:]
    acc = None
    for y_ref, (lo, hi) in zip(y_refs, splits):
        part = jnp.dot(y_ref[...], w_ref[lo:hi, :], preferred_element_type=F32)
        acc = part if acc is None else acc + part
    o_ref[...] = x_ref[...] + gt_ref[...] * (acc + b_ref[...])


def proj_residual(ys, w_bf16, b, x2d, mod, gate_chunk, seq, tm=512):
    r, d = x2d.shape
    kdim, n = w_bf16.shape
    tn = n
    tiles_per_seq = seq // tm
    splits, lo = [], 0
    for y in ys:
        splits.append((lo, lo + y.shape[1]))
        lo += y.shape[1]
    assert lo == kdim
    chunks_per_d = d // tn
    kern = functools.partial(_proj_res_kernel, n_in=len(ys), splits=tuple(splits))
    return pl.pallas_call(
        kern,
        grid=(r // tm, n // tn),
        in_specs=[pl.BlockSpec((tm, y.shape[1]), lambda i, j: (i, 0)) for y in ys] + [
            pl.BlockSpec((kdim, tn), lambda i, j: (0, j)),
            pl.BlockSpec((1, tn), lambda i, j: (0, j)),
            pl.BlockSpec((tm, tn), lambda i, j: (i, j)),
            pl.BlockSpec((None, 1, tn),
                         lambda i, j: (i // tiles_per_seq, 0, gate_chunk * chunks_per_d + j)),
        ],
        out_specs=pl.BlockSpec((tm, tn), lambda i, j: (i, j)),
        out_shape=jax.ShapeDtypeStruct((r, n), F32),
        compiler_params=_cparams(("parallel", "parallel")),
        name="proj_residual",
    )(*ys, w_bf16, b.reshape(1, n), x2d, mod)


def _na_kernel(q_ref, k_ref, v_ref, kc_ref, vc_ref, bias_ref, o_ref, *, scale, rows, kh):
    i = pl.program_id(2)
    nt = (((1,), (1,)), ((), ()))
    kc, vc = kc_ref[...], vc_ref[...]
    s_ctx = lax.dot_general(q_ref[...], kc, nt, preferred_element_type=F32) * scale
    def scores(rq):
        r = i * NA_QROWS + rq
        rs = jnp.clip(r - kh // 2, 0, rows - kh)
        start = pl.multiple_of(rs * GRID_W, GRID_W)
        qs = slice(rq * GRID_W, (rq + 1) * GRID_W)
        kw = k_ref[pl.ds(start, kh * GRID_W), :]
        s = lax.dot_general(q_ref[qs, :], kw, nt, preferred_element_type=F32) * scale + bias_ref[r - rs]
        return s, start

    def finish(rq, s, start):
        qs = slice(rq * GRID_W, (rq + 1) * GRID_W)
        vw = v_ref[pl.ds(start, kh * GRID_W), :]
        sc = s_ctx[qs, :]
        m = jnp.maximum(jnp.max(s, axis=-1, keepdims=True), jnp.max(sc, axis=-1, keepdims=True))
        p = jnp.exp(s - m)
        pc = jnp.exp(sc - m)
        l = jnp.sum(p, axis=-1, keepdims=True) + jnp.sum(pc, axis=-1, keepdims=True)
        o = (jnp.dot(p.astype(BF16), vw, preferred_element_type=F32)
             + jnp.dot(pc.astype(BF16), vc, preferred_element_type=F32))
        o_ref[qs, :] = (o / l).astype(o_ref.dtype)

    pending = [scores(rq) for rq in range(ATTN_LOOKAHEAD)]
    for rq in range(NA_QROWS):
        if rq + ATTN_LOOKAHEAD < NA_QROWS:
            pending.append(scores(rq + ATTN_LOOKAHEAD))
        finish(rq, *pending.pop(0))


def na_bias_table(rpb, rows):
    heads = rpb.shape[0]
    kh = min(NA_KH, rows)
    c = np.arange(GRID_W)
    dc = c[None, :] - c[:, None]
    col_start = np.clip(c - NA_KW // 2, 0, GRID_W - NA_KW)
    col_valid = (c[None, :] >= col_start[:, None]) & (c[None, :] < col_start[:, None] + NA_KW)
    col_idx = np.clip(dc + NA_KW - 1, 0, 2 * NA_KW - 2)
    toep = jnp.take(rpb, jnp.asarray(col_idx.reshape(-1)), axis=2).reshape(heads, 2 * NA_KH - 1, GRID_W, GRID_W)
    toep = jnp.where(jnp.asarray(col_valid)[None, None], toep, NEG).transpose(0, 2, 1, 3)
    tables = []
    for delta in range(kh):
        lo = NA_KH - 1 - delta
        tables.append(toep[:, :, lo:lo + kh, :].reshape(heads, GRID_W, kh * GRID_W))
    return jnp.stack(tables, axis=1)


def na_attention(qkv, qkv_ctx, bias, heads, q_blk0, k_blk0, v_blk0):
    b, s, _ = qkv.shape
    n_ctx = qkv_ctx.shape[1]
    rows = s // GRID_W
    kh = min(NA_KH, rows)
    tq = NA_QROWS * GRID_W
    kern = functools.partial(_na_kernel, scale=HEAD_DIM ** -0.5, rows=rows, kh=kh)
    return pl.pallas_call(
        kern,
        grid=(b, heads, rows // NA_QROWS),
        in_specs=[
            pl.BlockSpec((None, tq, HEAD_DIM), lambda bi, h, i: (bi, i, q_blk0 + h)),
            pl.BlockSpec((None, s, HEAD_DIM), lambda bi, h, i: (bi, 0, k_blk0 + h)),
            pl.BlockSpec((None, s, HEAD_DIM), lambda bi, h, i: (bi, 0, v_blk0 + h)),
            pl.BlockSpec((None, n_ctx, HEAD_DIM), lambda bi, h, i: (bi, 0, k_blk0 + h)),
            pl.BlockSpec((None, n_ctx, HEAD_DIM), lambda bi, h, i: (bi, 0, v_blk0 + h)),
            pl.BlockSpec((None, kh, GRID_W, kh * GRID_W), lambda bi, h, i: (h, 0, 0, 0)),
        ],
        out_specs=pl.BlockSpec((None, tq, HEAD_DIM), lambda bi, h, i: (bi, i, h)),
        out_shape=jax.ShapeDtypeStruct((b, s, heads * HEAD_DIM), BF16),
        compiler_params=_cparams(("parallel", "parallel", "arbitrary")),
        name="na_attention",
    )(qkv, qkv, qkv, qkv_ctx, qkv_ctx, bias)


def _band_kernel(q_ref, k_ref, v_ref, kc_ref, vc_ref, sink_ref, o_ref, kcat, vcat, *, scale, blk, group, seq):
    kvh = pl.program_id(1)
    n = pl.program_id(2)
    n_win = 3 * blk
    start = pl.multiple_of(jnp.clip((n - 1) * blk, 0, seq - n_win), blk)
    kcat[0:n_win, :] = k_ref[pl.ds(start, n_win), :]
    vcat[0:n_win, :] = v_ref[pl.ds(start, n_win), :]
    kcat[n_win:, :] = kc_ref[...]
    vcat[n_win:, :] = vc_ref[...]
    nk = kcat.shape[0]
    qpos = n * blk + lax.broadcasted_iota(jnp.int32, (blk, nk), 0)
    col = lax.broadcasted_iota(jnp.int32, (blk, nk), 1)
    kpos = start + col
    valid = (col >= n_win) | (jnp.abs(qpos - kpos) <= SW_WINDOW)
    kk, vv = kcat[...], vcat[...]
    def scores(g):
        hs = slice(g * HEAD_DIM, (g + 1) * HEAD_DIM)
        return lax.dot_general(q_ref[:, hs], kk, (((1,), (1,)), ((), ())), preferred_element_type=F32) * scale

    def finish(g, s):
        s = jnp.where(valid, s, NEG)
        sink = sink_ref[kvh * group + g]
        m = jnp.maximum(jnp.max(s, axis=-1, keepdims=True), sink)
        p = jnp.exp(s - m)
        l = jnp.sum(p, axis=-1, keepdims=True) + jnp.exp(sink - m)
        o = jnp.dot(p.astype(BF16), vv, preferred_element_type=F32) / l
        o_ref[:, g * HEAD_DIM:(g + 1) * HEAD_DIM] = o.astype(o_ref.dtype)

    pending = [scores(g) for g in range(ATTN_LOOKAHEAD)]
    for g in range(group):
        if g + ATTN_LOOKAHEAD < group:
            pending.append(scores(g + ATTN_LOOKAHEAD))
        finish(g, pending.pop(0))


def band_attention(qkv, qkv_ctx, sinks, q_heads, kv_heads, q_col0, k_col0, v_col0, blk=128):
    b, s, _ = qkv.shape
    n_ctx = qkv_ctx.shape[1]
    group = q_heads // kv_heads
    gw = group * HEAD_DIM
    nk = 3 * blk + n_ctx
    kern = functools.partial(_band_kernel, scale=HEAD_DIM ** -0.5, blk=blk, group=group, seq=s)
    return pl.pallas_call(
        kern,
        grid=(b, kv_heads, s // blk),
        in_specs=[
            pl.BlockSpec((None, blk, gw), lambda bi, kv, n: (bi, n, q_col0 // gw + kv)),
            pl.BlockSpec((None, s, HEAD_DIM), lambda bi, kv, n: (bi, 0, k_col0 // HEAD_DIM + kv)),
            pl.BlockSpec((None, s, HEAD_DIM), lambda bi, kv, n: (bi, 0, v_col0 // HEAD_DIM + kv)),
            pl.BlockSpec((None, n_ctx, HEAD_DIM), lambda bi, kv, n: (bi, 0, k_col0 // HEAD_DIM + kv)),
            pl.BlockSpec((None, n_ctx, HEAD_DIM), lambda bi, kv, n: (bi, 0, v_col0 // HEAD_DIM + kv)),
            pl.BlockSpec(memory_space=pltpu.SMEM),
        ],
        out_specs=pl.BlockSpec((None, blk, gw), lambda bi, kv, n: (bi, n, kv)),
        out_shape=jax.ShapeDtypeStruct((b, s, q_heads * HEAD_DIM), BF16),
        scratch_shapes=[pltpu.VMEM((nk, HEAD_DIM), BF16), pltpu.VMEM((nk, HEAD_DIM), BF16)],
        compiler_params=_cparams(("parallel", "parallel", "arbitrary")),
        name="band_attention",
    )(qkv, qkv, qkv, qkv_ctx, qkv_ctx, sinks)


def _ffn_kernel(x_ref, g_ref, sh_ref, sc_ref, gt_ref, wg_ref, wu_ref, wd_ref, o_ref, h_scr):
    f = pl.program_id(1)

    @pl.when(f == 0)
    def _():
        h_scr[...] = _norm_mod(x_ref[...], g_ref[...], sh_ref[...], sc_ref[...]).astype(BF16)
        o_ref[...] = jnp.zeros_like(o_ref)

    h = h_scr[...]
    gate = jnp.dot(h, wg_ref[...], preferred_element_type=F32)
    up = jnp.dot(h, wu_ref[...], preferred_element_type=F32)
    a = (_silu(gate) * up).astype(BF16)
    for c0 in range(0, o_ref.shape[1], FFN_OUT_CHUNK):
        cs = slice(c0, c0 + FFN_OUT_CHUNK)
        o_ref[:, cs] += jnp.dot(a, wd_ref[:, cs], preferred_element_type=F32)

    @pl.when(f == pl.num_programs(1) - 1)
    def _():
        o_ref[...] = x_ref[...] + gt_ref[...] * o_ref[...]


def dense_ffn(x2d, g, mod, seq, wg, wu, wd, tm=1024, tf=512):
    r, d = x2d.shape
    dff = wg.shape[1]
    tiles_per_seq = seq // tm
    row = lambda i, f: i // tiles_per_seq
    return pl.pallas_call(
        _ffn_kernel,
        grid=(r // tm, dff // tf),
        in_specs=[
            pl.BlockSpec((tm, d), lambda i, f: (i, 0), pipeline_mode=pl.Buffered(1)),
            pl.BlockSpec((1, d), lambda i, f: (0, 0)),
            pl.BlockSpec((None, 1, d), lambda i, f: (row(i, f), 0, 3)),
            pl.BlockSpec((None, 1, d), lambda i, f: (row(i, f), 0, 4)),
            pl.BlockSpec((None, 1, d), lambda i, f: (row(i, f), 0, 5)),
            pl.BlockSpec((d, tf), lambda i, f: (0, f)),
            pl.BlockSpec((d, tf), lambda i, f: (0, f)),
            pl.BlockSpec((tf, d), lambda i, f: (f, 0)),
        ],
        out_specs=pl.BlockSpec((tm, d), lambda i, f: (i, 0)),
        out_shape=jax.ShapeDtypeStruct((r, d), F32),
        scratch_shapes=[pltpu.VMEM((tm, d), BF16)],
        compiler_params=_cparams(("parallel", "arbitrary")),
        name="dense_ffn",
    )(x2d, g.reshape(1, d), mod, mod, mod, wg, wu, wd)


def _conv_kernel(prev_ref, cur_ref, next_ref, w_ref, b_ref, g_ref, beta_ref, o_ref, ubuf, cbuf, *, ts, rb, cb):
    i = pl.program_id(1)
    last = pl.num_programs(1) - 1
    d = cur_ref.shape[1]
    zeros = jnp.zeros((CONV_HALO, d), F32)

    @pl.when(i == 0)
    def _():
        ubuf[0, 0:CONV_HALO, :] = zeros

    @pl.when(i > 0)
    def _():
        ubuf[0, 0:CONV_HALO, :] = prev_ref[...]

    ubuf[0, CONV_HALO:CONV_HALO + ts, :] = cur_ref[...]

    @pl.when(i == last)
    def _():
        ubuf[0, CONV_HALO + ts:, :] = zeros

    @pl.when(i < last)
    def _():
        ubuf[0, CONV_HALO + ts:, :] = next_ref[...]

    off = CONV_HALO - CONV_WIDTH // 2
    rows_read = ts + (off + CONV_WIDTH - 1) // SUBLANES * SUBLANES
    for s in range(1, SUBLANES):
        for c0 in range(0, d, CONV_COPY_COLS):
            for a in range(0, rows_read, CONV_COPY_ROWS):
                n = min(CONV_COPY_ROWS, rows_read - a)
                ubuf[s, a:a + n, c0:c0 + CONV_COPY_COLS] = ubuf[0, a + s:a + s + n, c0:c0 + CONV_COPY_COLS]

    for c0 in range(0, d, cb):
        wt = w_ref[:, c0:c0 + cb]
        bias = b_ref[:, c0:c0 + cb]

        def row_body(rt, carry, c0=c0, wt=wt, bias=bias):
            r0 = pl.multiple_of(rt * rb, rb)
            acc = jnp.zeros((rb, cb), F32) + bias
            for k in range(CONV_WIDTH):
                s, q = (off + k) % SUBLANES, (off + k) // SUBLANES
                acc = acc + ubuf[s, pl.ds(r0 + q * SUBLANES, rb), c0:c0 + cb] * wt[k:k + 1, :]
            cbuf[pl.ds(r0, rb), c0:c0 + cb] = acc
            return carry

        lax.fori_loop(0, ts // rb, row_body, 0)

    v = cbuf[...]
    mu = jnp.mean(v, axis=-1, keepdims=True)
    cen = v - mu
    var = jnp.mean(cen * cen, axis=-1, keepdims=True)
    y = cen * lax.rsqrt(var + EPS) * g_ref[...] + beta_ref[...]
    o_ref[...] = _silu(y).astype(o_ref.dtype)


def conv_ln_swish(u, w_dw, b_dw, ln_g, ln_b, ts=256, rb=32, cb=256):
    b, s, d = u.shape
    hb = ts // CONV_HALO
    n_halo = s // CONV_HALO
    kern = functools.partial(_conv_kernel, ts=ts, rb=rb, cb=cb)
    return pl.pallas_call(
        kern,
        grid=(b, s // ts),
        in_specs=[
            pl.BlockSpec((None, CONV_HALO, d), lambda bi, i: (bi, jnp.maximum(i * hb - 1, 0), 0)),
            pl.BlockSpec((None, ts, d), lambda bi, i: (bi, i, 0)),
            pl.BlockSpec((None, CONV_HALO, d), lambda bi, i: (bi, jnp.minimum((i + 1) * hb, n_halo - 1), 0)),
            pl.BlockSpec((CONV_WIDTH, d), lambda bi, i: (0, 0)),
            pl.BlockSpec((1, d), lambda bi, i: (0, 0)),
            pl.BlockSpec((1, d), lambda bi, i: (0, 0)),
            pl.BlockSpec((1, d), lambda bi, i: (0, 0)),
        ],
        out_specs=pl.BlockSpec((None, ts, d), lambda bi, i: (bi, i, 0)),
        out_shape=jax.ShapeDtypeStruct((b, s, d), BF16),
        scratch_shapes=[pltpu.VMEM((SUBLANES, ts + 2 * CONV_HALO, d), F32), pltpu.VMEM((ts, d), F32)],
        compiler_params=_cparams(("parallel", "arbitrary")),
        name="conv_ln_swish",
    )(u, u, u, w_dw, b_dw.reshape(1, d), ln_g.reshape(1, d), ln_b.reshape(1, d))


def _bf16_bits(v):
    b = pltpu.bitcast(v, jnp.uint32)
    return (b + jnp.uint32(0x7FFF) + ((b >> 16) & jnp.uint32(1))) >> 16


def _router_kernel(x_ref, g_ref, sh_ref, sc_ref, wh_ref, wl_ref, hp_ref, idx_ref, wt_ref, *, n_experts):
    h = _norm_mod(x_ref[...], g_ref[...], sh_ref[...], sc_ref[...])
    half = h.shape[1] // 2
    hp_ref[...] = (_bf16_bits(h[:, half:]) << 16) | _bf16_bits(h[:, :half])
    h_hi = h.astype(BF16)
    h_lo = (h - h_hi.astype(F32)).astype(BF16)
    wh, wl = wh_ref[...], wl_ref[...]
    logits = (jnp.dot(h_hi, wh, preferred_element_type=F32)
              + (jnp.dot(h_hi, wl, preferred_element_type=F32) + jnp.dot(h_lo, wh, preferred_element_type=F32)))
    lane = lax.broadcasted_iota(jnp.int32, logits.shape, 1)
    big = logits.shape[1]
    logits = jnp.where(lane < n_experts, logits, -jnp.inf)
    m1 = jnp.max(logits, axis=-1, keepdims=True)
    i1 = jnp.min(jnp.where(logits == m1, lane, big), axis=-1, keepdims=True)
    rest = jnp.where(lane == i1, -jnp.inf, logits)
    m2 = jnp.max(rest, axis=-1, keepdims=True)
    i2 = jnp.min(jnp.where(rest == m2, lane, big), axis=-1, keepdims=True)
    e = jnp.exp(m2 - m1)
    w1 = 1.0 / (1.0 + e)
    w2 = e / (1.0 + e)
    idx_ref[...] = jnp.where(lane == 0, i1, jnp.where(lane == 1, i2, 0))
    wt_ref[...] = jnp.where(lane == 0, w1, jnp.where(lane == 1, w2, 0.0))


def router(x2d, g, mod, seq, w_router, tm=512):
    r, d = x2d.shape
    n_experts = w_router.shape[1]
    lanes = 128
    wpad = jnp.zeros((d, lanes), F32).at[:, :n_experts].set(w_router)
    wh = wpad.astype(BF16)
    wl = (wpad - wh.astype(F32)).astype(BF16)
    tiles_per_seq = seq // tm
    kern = functools.partial(_router_kernel, n_experts=n_experts)
    hp, idx, wt = pl.pallas_call(
        kern,
        grid=(r // tm,),
        in_specs=[
            pl.BlockSpec((tm, d), lambda i: (i, 0)),
            pl.BlockSpec((1, d), lambda i: (0, 0)),
            pl.BlockSpec((None, 1, d), lambda i: (i // tiles_per_seq, 0, 3)),
            pl.BlockSpec((None, 1, d), lambda i: (i // tiles_per_seq, 0, 4)),
            pl.BlockSpec((d, lanes), lambda i: (0, 0)),
            pl.BlockSpec((d, lanes), lambda i: (0, 0)),
        ],
        out_specs=[
            pl.BlockSpec((tm, d // 2), lambda i: (i, 0)),
            pl.BlockSpec((tm, lanes), lambda i: (i, 0)),
            pl.BlockSpec((tm, lanes), lambda i: (i, 0)),
        ],
        out_shape=[
            jax.ShapeDtypeStruct((r, d // 2), jnp.uint32),
            jax.ShapeDtypeStruct((r, lanes), jnp.int32),
            jax.ShapeDtypeStruct((r, lanes), F32),
        ],
        compiler_params=_cparams(("parallel",)),
        name="router",
    )(x2d, g.reshape(1, d), mod, mod, wh, wl)
    return hp, idx[:, :TOP_K], wt[:, :TOP_K]


def _issue_rows(table_ref, idx_ref, idx_base, dst_of_row, sem, row0, n_rows):
    def body(g, carry):
        for u in range(DMA_UNROLL):
            r = row0 + g * DMA_UNROLL + u
            pltpu.make_async_copy(table_ref.at[pl.ds(idx_ref[idx_base + r], 1)], dst_of_row(r), sem).start()
        return carry

    lax.fori_loop(0, n_rows // DMA_UNROLL, body, 0)


def _expert_kernel(te_ref, tr_ref, nt_ref, src_ref, hp_ref, wg_ref, wu_ref, wd_ref, o_ref, xbuf, x_scr, sem,
                   *, tm, rows_per_step):
    i = pl.program_id(0)
    f = pl.program_id(1)
    n_live = nt_ref[0]
    live = i < n_live
    slot = i % 2

    @pl.when(jnp.logical_not(live) & (f == 0))
    def _():
        o_ref[...] = jnp.zeros_like(o_ref)

    @pl.when((i == 0) & (f == 0))
    def _():
        _issue_rows(hp_ref, src_ref, 0, lambda r: xbuf.at[0, pl.ds(r, 1)], sem.at[0], 0, tm)

    @pl.when((i + 1 < n_live) & (f < tm // rows_per_step))
    def _():
        _issue_rows(hp_ref, src_ref, (i + 1) * tm, lambda r: xbuf.at[1 - slot, pl.ds(r, 1)], sem.at[1 - slot],
                    f * rows_per_step, rows_per_step)

    @pl.when(live & (f == 0))
    def _():
        pltpu.make_async_copy(hp_ref.at[pl.ds(0, tm)], xbuf.at[slot], sem.at[slot]).wait()
        xp = xbuf[slot]
        lo = pltpu.bitcast(xp << 16, F32)
        hi = pltpu.bitcast(xp & jnp.uint32(0xFFFF0000), F32)
        half = xp.shape[1]
        x_scr[:, :half] = lo.astype(BF16)
        x_scr[:, half:] = hi.astype(BF16)
        o_ref[...] = jnp.zeros_like(o_ref)

    def swiglu_rows(rs, wg, wu, wd):
        x = x_scr[rs, :]
        gate = jnp.dot(x, wg, preferred_element_type=F32)
        up = jnp.dot(x, wu, preferred_element_type=F32)
        a = (_silu(gate) * up).astype(BF16)
        for c0 in range(0, o_ref.shape[1], FFN_OUT_CHUNK):
            cs = slice(c0, c0 + FFN_OUT_CHUNK)
            o_ref[rs, cs] += jnp.dot(a, wd[:, cs], preferred_element_type=F32)

    rows_valid = tr_ref[i]
    sparse_rows = EXPERT_SPARSE_BLOCKS * EXPERT_ROW_BLOCK

    @pl.when(live & (rows_valid > sparse_rows))
    def _():
        swiglu_rows(slice(0, tm), wg_ref[...].astype(BF16), wu_ref[...].astype(BF16), wd_ref[...].astype(BF16))

    @pl.when(live & (rows_valid <= sparse_rows))
    def _():
        wg, wu, wd = wg_ref[...].astype(BF16), wu_ref[...].astype(BF16), wd_ref[...].astype(BF16)
        for r0 in range(0, sparse_rows, EXPERT_ROW_BLOCK):
            @pl.when(r0 < rows_valid)
            def _(r0=r0):
                swiglu_rows(slice(r0, r0 + EXPERT_ROW_BLOCK), wg, wu, wd)


def expert_ffn(hp, src, tile_expert, tile_rows, n_tiles, wg, wu, wd, tm=1024, tf=256):
    half = hp.shape[1]
    r = src.shape[0]
    d = 2 * half
    dff = wg.shape[2]
    nf = dff // tf
    fetch_steps = next(n for n in range(min(nf, tm // DMA_UNROLL), 0, -1) if (tm // DMA_UNROLL) % n == 0)
    rows_per_step = tm // fetch_steps

    def live_tile(i, nt):
        return jnp.minimum(i, nt[0] - 1)

    def wf(i, f, nt):
        return jnp.where(i < nt[0], f, nf - 1)

    kern = functools.partial(_expert_kernel, tm=tm, rows_per_step=rows_per_step)
    return pl.pallas_call(
        kern,
        grid_spec=pltpu.PrefetchScalarGridSpec(
            num_scalar_prefetch=4,
            grid=(r // tm, nf),
            in_specs=[
                pl.BlockSpec(memory_space=pl.ANY),
                pl.BlockSpec((None, d, tf), lambda i, f, te, tr, nt, src: (te[live_tile(i, nt)], 0, wf(i, f, nt))),
                pl.BlockSpec((None, d, tf), lambda i, f, te, tr, nt, src: (te[live_tile(i, nt)], 0, wf(i, f, nt))),
                pl.BlockSpec((None, tf, d), lambda i, f, te, tr, nt, src: (te[live_tile(i, nt)], wf(i, f, nt), 0)),
            ],
            out_specs=pl.BlockSpec((tm, d), lambda i, f, te, tr, nt, src: (i, 0)),
            scratch_shapes=[pltpu.VMEM((2, tm, half), jnp.uint32), pltpu.VMEM((tm, d), BF16),
                            pltpu.SemaphoreType.DMA((2,))],
        ),
        out_shape=jax.ShapeDtypeStruct((r, d), F32),
        compiler_params=_cparams(("arbitrary", "arbitrary")),
        name="expert_ffn",
    )(tile_expert, tile_rows, n_tiles, src, hp, wg, wu, wd)


def _combine_kernel(pos_ref, y_ref, x_ref, w_ref, gt_ref, fg_ref, o_ref, buf, sem, *, tt):
    i = pl.program_id(0)
    slot = i % 2

    def fetch(tile, dst_slot):
        for k in range(TOP_K):
            _issue_rows(y_ref, pos_ref, tile * tt * TOP_K + k * tt, lambda r, k=k: buf.at[dst_slot, k, pl.ds(r, 1)],
                        sem.at[dst_slot], 0, tt)

    @pl.when(i == 0)
    def _():
        fetch(0, 0)

    @pl.when(i + 1 < pl.num_programs(0))
    def _():
        fetch(i + 1, 1 - slot)

    for k in range(TOP_K):
        pltpu.make_async_copy(y_ref.at[pl.ds(0, tt)], buf.at[slot, k], sem.at[slot]).wait()
    w = w_ref[...]
    y = buf[slot, 0] * w[:, 0:1]
    for k in range(1, TOP_K):
        y = y + buf[slot, k] * w[:, k:k + 1]
    x = x_ref[...] + gt_ref[...] * y
    ms = jnp.mean(x * x, axis=-1, keepdims=True)
    o_ref[...] = x * lax.rsqrt(ms + EPS) * fg_ref[...]


def combine_final(y_sorted, pos_kmajor, top_w, x2d, mod, seq, final_g, tt=256):
    r, d = x2d.shape
    tiles_per_seq = seq // tt
    kern = functools.partial(_combine_kernel, tt=tt)
    return pl.pallas_call(
        kern,
        grid_spec=pltpu.PrefetchScalarGridSpec(
            num_scalar_prefetch=1,
            grid=(r // tt,),
            in_specs=[
                pl.BlockSpec(memory_space=pl.ANY),
                pl.BlockSpec((tt, d), lambda i, pos: (i, 0)),
                pl.BlockSpec((tt, TOP_K), lambda i, pos: (i, 0)),
                pl.BlockSpec((None, 1, d), lambda i, pos: (i // tiles_per_seq, 0, 5)),
                pl.BlockSpec((1, d), lambda i, pos: (0, 0)),
            ],
            out_specs=pl.BlockSpec((tt, d), lambda i, pos: (i, 0)),
            scratch_shapes=[pltpu.VMEM((2, TOP_K, tt, d), F32), pltpu.SemaphoreType.DMA((2,))],
        ),
        out_shape=jax.ShapeDtypeStruct((r, d), F32),
        compiler_params=_cparams(("arbitrary",)),
        name="combine_final",
    )(pos_kmajor, y_sorted, x2d, top_w, mod, final_g.reshape(1, d))


def dispatch_plan(top_i, n_experts, tm, tt):
    t = top_i.shape[0]
    flat_e = top_i.reshape(-1)
    onehot = (flat_e[:, None] == jnp.arange(n_experts)[None, :]).astype(jnp.int32)
    rank = jnp.cumsum(onehot, axis=0) - onehot
    counts = jnp.sum(onehot, axis=0)
    tiles = (counts + tm - 1) // tm
    tile_end = jnp.cumsum(tiles)
    group_start = (tile_end - tiles) * tm
    pos = jnp.sum(onehot * (group_start[None, :] + rank), axis=1)
    n_rows = (t * TOP_K // tm + n_experts) * tm
    src = jnp.zeros((n_rows,), jnp.int32).at[pos].set(jnp.arange(t * TOP_K, dtype=jnp.int32) // TOP_K)
    tile_ids = jnp.arange(n_rows // tm)
    tile_expert = jnp.minimum(jnp.sum(tile_ids[:, None] >= tile_end[None, :], axis=1), n_experts - 1)
    tile_in_group = tile_ids - (tile_end - tiles)[tile_expert]
    tile_rows = jnp.where(tile_ids < tile_end[-1], jnp.clip(counts[tile_expert] - tile_in_group * tm, 0, tm), 0)
    pos_kmajor = pos.astype(jnp.int32).reshape(t // tt, tt, TOP_K).transpose(0, 2, 1).reshape(-1)
    return src, pos_kmajor, tile_expert.astype(jnp.int32), tile_rows.astype(jnp.int32), \
        tile_end[-1:].astype(jnp.int32)


def rope_tables(seq):
    t = np.arange(seq)
    row, col = t // GRID_W, t % GRID_W
    quarter = HEAD_DIM // 4
    inv = ROPE_BASE ** (-np.arange(0, 2 * quarter, 2, dtype=np.float32) / (2 * quarter))
    lane = np.arange(HEAD_DIM)
    pos = np.where(lane[None, :] < HEAD_DIM // 2, row[:, None], col[:, None]).astype(np.float32)
    ang = jnp.asarray(pos) * jnp.asarray(inv[lane % quarter])[None, :]
    cos, sin = jnp.cos(ang), jnp.sin(ang)
    first = jnp.asarray(((lane // quarter) % 2 == 0)[None, :])
    return cos, jnp.where(first, -sin, 0.0), jnp.where(first, 0.0, sin)


def kernel(x, c, ctx, c_ctx, ada_w, ada_b, norm_g, final_g, ab_w_in, ab_rpb, ab_sinks, ab_w_out,
           ffn_w_gate, ffn_w_up, ffn_w_down, conv_w_pw1, conv_b_pw1, conv_w_dw, conv_b_dw,
           conv_ln_g, conv_ln_b, conv_w_pw2, conv_b_pw2, moe_w_router, moe_w_gate, moe_w_up, moe_w_down):
    b, s, d = x.shape
    n_ctx = ctx.shape[1]
    depth = ada_w.shape[0]
    assert depth == 2, "layer schedule below is written for one even and one odd layer"
    na_w = ab_rpb.shape[1] * HEAD_DIM
    sw_q = ab_sinks.shape[1] * HEAD_DIM
    in_w = ab_w_in.shape[2]
    sw_kv = (in_w - 3 * na_w - sw_q) // 2
    n_experts = moe_w_router.shape[2]

    mod_rows = 8
    cvec = jnp.zeros((mod_rows, d), F32).at[:b].set(c).at[b].set(c_ctx)
    mods = ada_mod_all(cvec, ada_w, ada_b).reshape(depth, mod_rows, 1, N_MOD * d)

    x2d = x.reshape(b * s, d)
    ctx2d = ctx.reshape(b * n_ctx, d)

    mod0 = mods[0]
    w_in = ab_w_in[0].astype(BF16)
    tabs = rope_tables(s)
    rope_lo, rope_hi = 3 * na_w, 3 * na_w + sw_q + sw_kv
    tm_in = min(1024, s)
    qkv = in_proj(x2d, norm_g[0, 0], mod0, lambda i: i // (s // tm_in), w_in, tabs, rope_lo, rope_hi, s, tm_in)
    ctx_tabs = tuple(t[:n_ctx] for t in tabs)
    qkv_ctx = in_proj(ctx2d, norm_g[0, 0], mod0, lambda i: b, w_in, ctx_tabs, 0, 0, n_ctx, n_ctx)
    qkv = qkv.reshape(b, s, in_w)
    qkv_ctx = qkv_ctx.reshape(b, n_ctx, in_w)

    na_heads = na_w // HEAD_DIM
    bias = na_bias_table(ab_rpb[0], s // GRID_W)
    y_a = na_attention(qkv, qkv_ctx, bias, na_heads, 0, na_heads, 2 * na_heads)
    y_b = band_attention(qkv, qkv_ctx, ab_sinks[0], sw_q // HEAD_DIM, sw_kv // HEAD_DIM,
                         3 * na_w, 3 * na_w + sw_q, 3 * na_w + sw_q + sw_kv)
    zero_d = jnp.zeros((d,), F32)
    x2d = proj_residual([y_a.reshape(b * s, na_w), y_b.reshape(b * s, sw_q)], ab_w_out[0].astype(BF16),
                        zero_d, x2d, mod0, 2, s)
    x2d = dense_ffn(x2d, norm_g[0, 1], mod0, s, ffn_w_gate[0].astype(BF16), ffn_w_up[0].astype(BF16),
                    ffn_w_down[0].astype(BF16))

    mod1 = mods[1]
    u = pw1_glu(x2d, norm_g[1, 0], mod1, s, conv_w_pw1[0].astype(BF16), conv_b_pw1[0])
    v = conv_ln_swish(u.reshape(b, s, d), conv_w_dw[0], conv_b_dw[0], conv_ln_g[0], conv_ln_b[0])
    x2d = proj_residual([v.reshape(b * s, d)], conv_w_pw2[0].astype(BF16), conv_b_pw2[0], x2d, mod1, 2, s)

    hp, top_i, top_w = router(x2d, norm_g[1, 1], mod1, s, moe_w_router[0])
    tm_e, tt = 1024, 256
    src, pos_kmajor, tile_expert, tile_rows, n_tiles = dispatch_plan(top_i, n_experts, tm_e, tt)
    y_sorted = expert_ffn(hp, src, tile_expert, tile_rows, n_tiles, moe_w_gate[0], moe_w_up[0], moe_w_down[0],
                          tm=tm_e)
    out = combine_final(y_sorted, pos_kmajor, top_w, x2d, mod1, s, final_g, tt=tt)
    return out.reshape(b, s, d)
```

```python
import functools

import numpy as np
import jax
import jax.numpy as jnp
from jax import lax
from jax.experimental import pallas as pl
from jax.experimental.pallas import tpu as pltpu

F32 = jnp.float32
BF16 = jnp.bfloat16
F8 = jnp.float8_e4m3fn
F8_MAX = 448.0
F8_TINY = 1e-30

GRID_W = 64
HEAD_DIM = 128
NA_KH = 8
NA_KW = 16
SW_WINDOW = 128
ROPE_BASE = 10000.0
CONV_WIDTH = 31
N_MOD = 6
TOP_K = 2
EPS = 1e-6
NEG = -1e30

VMEM_LIMIT = 56 * 1024 * 1024
NA_QROWS = 8
CONV_HALO = 16
CONV_COPY_ROWS, CONV_COPY_COLS = 64, 512
FFN_OUT_CHUNK = 512
SUBLANES = 8
EXPERT_ROW_BLOCK = 256
EXPERT_SPARSE_BLOCKS = 2
DMA_UNROLL = 8
ATTN_LOOKAHEAD = 3


def _pick_tile(n, candidates):
    return next(t for t in candidates if n % t == 0)


def _cparams(sem):
    return pltpu.CompilerParams(dimension_semantics=sem, vmem_limit_bytes=VMEM_LIMIT)


def _silu(v):
    return v / (1.0 + jnp.exp(-v))


def _sigmoid(v):
    return 1.0 / (1.0 + jnp.exp(-v))


def _norm_mod(x, g, sh, sc):
    ms = jnp.mean(x * x, axis=-1, keepdims=True)
    y = x * lax.rsqrt(ms + EPS) * g
    return y * (1.0 + sc) + sh


def _ada_kernel(c_ref, w_ref, b_ref, o_ref):
    s = _silu(c_ref[...]).astype(BF16)
    acc = jnp.dot(s, w_ref[...].astype(BF16), preferred_element_type=F32)
    o_ref[...] = acc + b_ref[...]


def ada_mod_all(cvec, ada_w, ada_b, tn=1024):
    depth, d, n = ada_w.shape
    rows = cvec.shape[0]
    return pl.pallas_call(
        _ada_kernel,
        grid=(depth, n // tn),
        in_specs=[
            pl.BlockSpec((rows, d), lambda l, j: (0, 0)),
            pl.BlockSpec((None, d, tn), lambda l, j: (l, 0, j)),
            pl.BlockSpec((None, 1, tn), lambda l, j: (l, 0, j)),
        ],
        out_specs=pl.BlockSpec((None, rows, tn), lambda l, j: (l, 0, j)),
        out_shape=jax.ShapeDtypeStruct((depth, rows, n), F32),
        compiler_params=_cparams(("parallel", "parallel")),
        name="ada_mod",
    )(cvec, ada_w, ada_b.reshape(depth, 1, n))


def _rope_head(v, cos, sa, sb):
    return v * cos + pltpu.roll(v, 96, 1) * sa + pltpu.roll(v, 32, 1) * sb


def _inproj_kernel(x_ref, g_ref, sh_ref, sc_ref, w_ref, cos_ref, sa_ref, sb_ref, o_ref, h_scr,
                   *, rope_lo, rope_hi, tn):
    j = pl.program_id(1)

    @pl.when(j == 0)
    def _():
        h_scr[...] = _norm_mod(x_ref[...], g_ref[...], sh_ref[...], sc_ref[...]).astype(BF16)

    acc = jnp.dot(h_scr[...], w_ref[...], preferred_element_type=F32)
    heads = tn // HEAD_DIM
    first_tile = rope_lo // tn
    last_tile = (rope_hi - 1) // tn if rope_hi > rope_lo else -1

    def plain():
        o_ref[...] = acc.astype(o_ref.dtype)

    if rope_hi <= rope_lo:
        plain()
        return

    pl.when((j < first_tile) | (j > last_tile))(plain)

    for t in range(first_tile, last_tile + 1):
        def roped(t=t):
            cos, sa, sb = cos_ref[...], sa_ref[...], sb_ref[...]
            for hh in range(heads):
                col = t * tn + hh * HEAD_DIM
                v = acc[:, hh * HEAD_DIM:(hh + 1) * HEAD_DIM]
                if rope_lo <= col < rope_hi:
                    v = _rope_head(v, cos, sa, sb)
                o_ref[:, hh * HEAD_DIM:(hh + 1) * HEAD_DIM] = v.astype(o_ref.dtype)
        pl.when(j == t)(roped)


def in_proj(x2d, g, mod, mod_row_of_tile, w_bf16, rope_tabs, rope_lo, rope_hi, seq, tm):
    r, d = x2d.shape
    n = w_bf16.shape[1]
    tn = _pick_tile(n, (512, 256, 128))
    cos, sa, sb = rope_tabs
    tiles_per_seq = seq // tm
    tab_spec = pl.BlockSpec((tm, HEAD_DIM), lambda i, j: (i % tiles_per_seq, 0))
    kern = functools.partial(_inproj_kernel, rope_lo=rope_lo, rope_hi=rope_hi, tn=tn)
    return pl.pallas_call(
        kern,
        grid=(r // tm, n // tn),
        in_specs=[
            pl.BlockSpec((tm, d), lambda i, j: (i, 0)),
            pl.BlockSpec((1, d), lambda i, j: (0, 0)),
            pl.BlockSpec((None, 1, d), lambda i, j: (mod_row_of_tile(i), 0, 0)),
            pl.BlockSpec((None, 1, d), lambda i, j: (mod_row_of_tile(i), 0, 1)),
            pl.BlockSpec((d, tn), lambda i, j: (0, j)),
            tab_spec, tab_spec, tab_spec,
        ],
        out_specs=pl.BlockSpec((tm, tn), lambda i, j: (i, j)),
        out_shape=jax.ShapeDtypeStruct((r, n), BF16),
        scratch_shapes=[pltpu.VMEM((tm, d), BF16)],
        compiler_params=_cparams(("parallel", "arbitrary")),
        name="in_proj",
    )(x2d, g.reshape(1, d), mod, mod, w_bf16, cos, sa, sb)


def _glu_kernel(x_ref, g_ref, sh_ref, sc_ref, wa_ref, wg_ref, ba_ref, bg_ref, o_ref, h_scr):
    j = pl.program_id(1)

    @pl.when(j == 0)
    def _():
        h_scr[...] = _norm_mod(x_ref[...], g_ref[...], sh_ref[...], sc_ref[...]).astype(BF16)

    h = h_scr[...]
    a = jnp.dot(h, wa_ref[...], preferred_element_type=F32) + ba_ref[...]
    gate = jnp.dot(h, wg_ref[...], preferred_element_type=F32) + bg_ref[...]
    o_ref[...] = a * _sigmoid(gate)


def pw1_glu(x2d, g, mod, seq, w_bf16, b, tm=1024, tn=512):
    r, d = x2d.shape
    half = w_bf16.shape[1] // 2
    nj = half // tn
    tiles_per_seq = seq // tm
    b2 = b.reshape(1, 2 * half)
    return pl.pallas_call(
        _glu_kernel,
        grid=(r // tm, nj),
        in_specs=[
            pl.BlockSpec((tm, d), lambda i, j: (i, 0)),
            pl.BlockSpec((1, d), lambda i, j: (0, 0)),
            pl.BlockSpec((None, 1, d), lambda i, j: (i // tiles_per_seq, 0, 0)),
            pl.BlockSpec((None, 1, d), lambda i, j: (i // tiles_per_seq, 0, 1)),
            pl.BlockSpec((d, tn), lambda i, j: (0, j)),
            pl.BlockSpec((d, tn), lambda i, j: (0, j + nj)),
            pl.BlockSpec((1, tn), lambda i, j: (0, j)),
            pl.BlockSpec((1, tn), lambda i, j: (0, j + nj)),
        ],
        out_specs=pl.BlockSpec((tm, tn), lambda i, j: (i, j)),
        out_shape=jax.ShapeDtypeStruct((r, half), F32),
        scratch_shapes=[pltpu.VMEM((tm, d), BF16)],
        compiler_params=_cparams(("parallel", "arbitrary")),
        name="pw1_glu",
    )(x2d, g.reshape(1, d), mod, mod, w_bf16, w_bf16, b2, b2)


def _proj_res_kernel(*refs, n_in, splits):
    y_refs = refs[:n_in]
    w_ref, b_ref, x_ref, gt_ref, o_ref = refs[n_in:]
    acc = None
    for y_ref, (lo, hi) in zip(y_refs, splits):
        part = jnp.dot(y_ref[...], w_ref[lo:hi, :], preferred_element_type=F32)
        acc = part if acc is None else acc + part
    o_ref[...] = x_ref[...] + gt_ref[...] * (acc + b_ref[...])


def proj_residual(ys, w_bf16, b, x2d, mod, gate_chunk, seq, tm=512):
    r, d = x2d.shape
    kdim, n = w_bf16.shape
    tn = n
    tiles_per_seq = seq // tm
    splits, lo = [], 0
    for y in ys:
        splits.append((lo, lo + y.shape[1]))
        lo += y.shape[1]
    assert lo == kdim
    chunks_per_d = d // tn
    kern = functools.partial(_proj_res_kernel, n_in=len(ys), splits=tuple(splits))
    return pl.pallas_call(
        kern,
        grid=(r // tm, n // tn),
        in_specs=[pl.BlockSpec((tm, y.shape[1]), lambda i, j: (i, 0)) for y in ys] + [
            pl.BlockSpec((kdim, tn), lambda i, j: (0, j)),
            pl.BlockSpec((1, tn), lambda i, j: (0, j)),
            pl.BlockSpec((tm, tn), lambda i, j: (i, j)),
            pl.BlockSpec((None, 1, tn),
                         lambda i, j: (i // tiles_per_seq, 0, gate_chunk * chunks_per_d + j)),
        ],
        out_specs=pl.BlockSpec((tm, tn), lambda i, j: (i, j)),
        out_shape=jax.ShapeDtypeStruct((r, n), F32),
        compiler_params=_cparams(("parallel", "parallel")),
        name="proj_residual",
    )(*ys, w_bf16, b.reshape(1, n), x2d, mod)


def _na_kernel(q_ref, k_ref, v_ref, kc_ref, vc_ref, bias_ref, o_ref, *, scale, rows, kh):
    i = pl.program_id(2)
    nt = (((1,), (1,)), ((), ()))
    kc, vc = kc_ref[...], vc_ref[...]
    s_ctx = lax.dot_general(q_ref[...], kc, nt, preferred_element_type=F32) * scale
    def scores(rq):
        r = i * NA_QROWS + rq
        rs = jnp.clip(r - kh // 2, 0, rows - kh)
        start = pl.multiple_of(rs * GRID_W, GRID_W)
        qs = slice(rq * GRID_W, (rq + 1) * GRID_W)
        kw = k_ref[pl.ds(start, kh * GRID_W), :]
        s = lax.dot_general(q_ref[qs, :], kw, nt, preferred_element_type=F32) * scale + bias_ref[r - rs]
        return s, start

    def finish(rq, s, start):
        qs = slice(rq * GRID_W, (rq + 1) * GRID_W)
        vw = v_ref[pl.ds(start, kh * GRID_W), :]
        sc = s_ctx[qs, :]
        m = jnp.maximum(jnp.max(s, axis=-1, keepdims=True), jnp.max(sc, axis=-1, keepdims=True))
        p = jnp.exp(s - m)
        pc = jnp.exp(sc - m)
        l = jnp.sum(p, axis=-1, keepdims=True) + jnp.sum(pc, axis=-1, keepdims=True)
        o = (jnp.dot(p.astype(BF16), vw, preferred_element_type=F32)
             + jnp.dot(pc.astype(BF16), vc, preferred_element_type=F32))
        o_ref[qs, :] = (o / l).astype(o_ref.dtype)

    pending = [scores(rq) for rq in range(ATTN_LOOKAHEAD)]
    for rq in range(NA_QROWS):
        if rq + ATTN_LOOKAHEAD < NA_QROWS:
            pending.append(scores(rq + ATTN_LOOKAHEAD))
        finish(rq, *pending.pop(0))


def na_bias_table(rpb, rows):
    heads = rpb.shape[0]
    kh = min(NA_KH, rows)
    c = np.arange(GRID_W)
    dc = c[None, :] - c[:, None]
    col_start = np.clip(c - NA_KW // 2, 0, GRID_W - NA_KW)
    col_valid = (c[None, :] >= col_start[:, None]) & (c[None, :] < col_start[:, None] + NA_KW)
    col_idx = np.clip(dc + NA_KW - 1, 0, 2 * NA_KW - 2)
    toep = jnp.take(rpb, jnp.asarray(col_idx.reshape(-1)), axis=2).reshape(heads, 2 * NA_KH - 1, GRID_W, GRID_W)
    toep = jnp.where(jnp.asarray(col_valid)[None, None], toep, NEG).transpose(0, 2, 1, 3)
    tables = []
    for delta in range(kh):
        lo = NA_KH - 1 - delta
        tables.append(toep[:, :, lo:lo + kh, :].reshape(heads, GRID_W, kh * GRID_W))
    return jnp.stack(tables, axis=1)


def na_attention(qkv, qkv_ctx, bias, heads, q_blk0, k_blk0, v_blk0):
    b, s, _ = qkv.shape
    n_ctx = qkv_ctx.shape[1]
    rows = s // GRID_W
    kh = min(NA_KH, rows)
    tq = NA_QROWS * GRID_W
    kern = functools.partial(_na_kernel, scale=HEAD_DIM ** -0.5, rows=rows, kh=kh)
    return pl.pallas_call(
        kern,
        grid=(b, heads, rows // NA_QROWS),
        in_specs=[
            pl.BlockSpec((None, tq, HEAD_DIM), lambda bi, h, i: (bi, i, q_blk0 + h)),
            pl.BlockSpec((None, s, HEAD_DIM), lambda bi, h, i: (bi, 0, k_blk0 + h)),
            pl.BlockSpec((None, s, HEAD_DIM), lambda bi, h, i: (bi, 0, v_blk0 + h)),
            pl.BlockSpec((None, n_ctx, HEAD_DIM), lambda bi, h, i: (bi, 0, k_blk0 + h)),
            pl.BlockSpec((None, n_ctx, HEAD_DIM), lambda bi, h, i: (bi, 0, v_blk0 + h)),
            pl.BlockSpec((None, kh, GRID_W, kh * GRID_W), lambda bi, h, i: (h, 0, 0, 0)),
        ],
        out_specs=pl.BlockSpec((None, tq, HEAD_DIM), lambda bi, h, i: (bi, i, h)),
        out_shape=jax.ShapeDtypeStruct((b, s, heads * HEAD_DIM), BF16),
        compiler_params=_cparams(("parallel", "parallel", "arbitrary")),
        name="na_attention",
    )(qkv, qkv, qkv, qkv_ctx, qkv_ctx, bias)


def _band_kernel(q_ref, k_ref, v_ref, kc_ref, vc_ref, sink_ref, o_ref, kcat, vcat, *, scale, blk, group, seq):
    kvh = pl.program_id(1)
    n = pl.program_id(2)
    n_win = 3 * blk
    start = pl.multiple_of(jnp.clip((n - 1) * blk, 0, seq - n_win), blk)
    kcat[0:n_win, :] = k_ref[pl.ds(start, n_win), :]
    vcat[0:n_win, :] = v_ref[pl.ds(start, n_win), :]
    kcat[n_win:, :] = kc_ref[...]
    vcat[n_win:, :] = vc_ref[...]
    nk = kcat.shape[0]
    qpos = n * blk + lax.broadcasted_iota(jnp.int32, (blk, nk), 0)
    col = lax.broadcasted_iota(jnp.int32, (blk, nk), 1)
    kpos = start + col
    valid = (col >= n_win) | (jnp.abs(qpos - kpos) <= SW_WINDOW)
    kk, vv = kcat[...], vcat[...]
    def scores(g):
        hs = slice(g * HEAD_DIM, (g + 1) * HEAD_DIM)
        return lax.dot_general(q_ref[:, hs], kk, (((1,), (1,)), ((), ())), preferred_element_type=F32) * scale

    def finish(g, s):
        s = jnp.where(valid, s, NEG)
        sink = sink_ref[kvh * group + g]
        m = jnp.maximum(jnp.max(s, axis=-1, keepdims=True), sink)
        p = jnp.exp(s - m)
        l = jnp.sum(p, axis=-1, keepdims=True) + jnp.exp(sink - m)
        o = jnp.dot(p.astype(BF16), vv, preferred_element_type=F32) / l
        o_ref[:, g * HEAD_DIM:(g + 1) * HEAD_DIM] = o.astype(o_ref.dtype)

    pending = [scores(g) for g in range(ATTN_LOOKAHEAD)]
    for g in range(group):
        if g + ATTN_LOOKAHEAD < group:
            pending.append(scores(g + ATTN_LOOKAHEAD))
        finish(g, pending.pop(0))


def band_attention(qkv, qkv_ctx, sinks, q_heads, kv_heads, q_col0, k_col0, v_col0, blk=128):
    b, s, _ = qkv.shape
    n_ctx = qkv_ctx.shape[1]
    group = q_heads // kv_heads
    gw = group * HEAD_DIM
    nk = 3 * blk + n_ctx
    kern = functools.partial(_band_kernel, scale=HEAD_DIM ** -0.5, blk=blk, group=group, seq=s)
    return pl.pallas_call(
        kern,
        grid=(b, kv_heads, s // blk),
        in_specs=[
            pl.BlockSpec((None, blk, gw), lambda bi, kv, n: (bi, n, q_col0 // gw + kv)),
            pl.BlockSpec((None, s, HEAD_DIM), lambda bi, kv, n: (bi, 0, k_col0 // HEAD_DIM + kv)),
            pl.BlockSpec((None, s, HEAD_DIM), lambda bi, kv, n: (bi, 0, v_col0 // HEAD_DIM + kv)),
            pl.BlockSpec((None, n_ctx, HEAD_DIM), lambda bi, kv, n: (bi, 0, k_col0 // HEAD_DIM + kv)),
            pl.BlockSpec((None, n_ctx, HEAD_DIM), lambda bi, kv, n: (bi, 0, v_col0 // HEAD_DIM + kv)),
            pl.BlockSpec(memory_space=pltpu.SMEM),
        ],
        out_specs=pl.BlockSpec((None, blk, gw), lambda bi, kv, n: (bi, n, kv)),
        out_shape=jax.ShapeDtypeStruct((b, s, q_heads * HEAD_DIM), BF16),
        scratch_shapes=[pltpu.VMEM((nk, HEAD_DIM), BF16), pltpu.VMEM((nk, HEAD_DIM), BF16)],
        compiler_params=_cparams(("parallel", "parallel", "arbitrary")),
        name="band_attention",
    )(qkv, qkv, qkv, qkv_ctx, qkv_ctx, sinks)


def _quantize(v):
    amax = jnp.maximum(jnp.max(jnp.abs(v)), F8_TINY)
    return (v * (F8_MAX / amax)).astype(F8), amax * (1.0 / F8_MAX)


def _quantize_cols(w):
    amax = jnp.maximum(jnp.max(jnp.abs(w), axis=0, keepdims=True), F8_TINY)
    return (w * (F8_MAX / amax)).astype(F8), amax * (1.0 / F8_MAX)


def _ffn_kernel(x_ref, g_ref, sh_ref, sc_ref, gt_ref, wg_ref, wu_ref, wd_ref, o_ref, h_scr):
    f = pl.program_id(1)

    @pl.when(f == 0)
    def _():
        h_scr[...] = _norm_mod(x_ref[...], g_ref[...], sh_ref[...], sc_ref[...]).astype(BF16)
        o_ref[...] = jnp.zeros_like(o_ref)

    h = h_scr[...]
    gate = jnp.dot(h, wg_ref[...], preferred_element_type=F32)
    up = jnp.dot(h, wu_ref[...], preferred_element_type=F32)
    a = (_silu(gate) * up).astype(BF16)
    for c0 in range(0, o_ref.shape[1], FFN_OUT_CHUNK):
        cs = slice(c0, c0 + FFN_OUT_CHUNK)
        o_ref[:, cs] += jnp.dot(a, wd_ref[:, cs], preferred_element_type=F32)

    @pl.when(f == pl.num_programs(1) - 1)
    def _():
        o_ref[...] = x_ref[...] + gt_ref[...] * o_ref[...]


def dense_ffn(x2d, g, mod, seq, wg, wu, wd, tm=1024, tf=512):
    r, d = x2d.shape
    dff = wg.shape[1]
    tiles_per_seq = seq // tm
    row = lambda i, f: i // tiles_per_seq
    return pl.pallas_call(
        _ffn_kernel,
        grid=(r // tm, dff // tf),
        in_specs=[
            pl.BlockSpec((tm, d), lambda i, f: (i, 0), pipeline_mode=pl.Buffered(1)),
            pl.BlockSpec((1, d), lambda i, f: (0, 0)),
            pl.BlockSpec((None, 1, d), lambda i, f: (row(i, f), 0, 3)),
            pl.BlockSpec((None, 1, d), lambda i, f: (row(i, f), 0, 4)),
            pl.BlockSpec((None, 1, d), lambda i, f: (row(i, f), 0, 5)),
            pl.BlockSpec((d, tf), lambda i, f: (0, f)),
            pl.BlockSpec((d, tf), lambda i, f: (0, f)),
            pl.BlockSpec((tf, d), lambda i, f: (f, 0)),
        ],
        out_specs=pl.BlockSpec((tm, d), lambda i, f: (i, 0)),
        out_shape=jax.ShapeDtypeStruct((r, d), F32),
        scratch_shapes=[pltpu.VMEM((tm, d), BF16)],
        compiler_params=_cparams(("parallel", "arbitrary")),
        name="dense_ffn",
    )(x2d, g.reshape(1, d), mod, mod, mod, wg, wu, wd)


def _conv_kernel(prev_ref, cur_ref, next_ref, w_ref, b_ref, g_ref, beta_ref, o_ref, ubuf, cbuf, *, ts, rb, cb):
    i = pl.program_id(1)
    last = pl.num_programs(1) - 1
    d = cur_ref.shape[1]
    zeros = jnp.zeros((CONV_HALO, d), F32)

    @pl.when(i == 0)
    def _():
        ubuf[0, 0:CONV_HALO, :] = zeros

    @pl.when(i > 0)
    def _():
        ubuf[0, 0:CONV_HALO, :] = prev_ref[...]

    ubuf[0, CONV_HALO:CONV_HALO + ts, :] = cur_ref[...]

    @pl.when(i == last)
    def _():
        ubuf[0, CONV_HALO + ts:, :] = zeros

    @pl.when(i < last)
    def _():
        ubuf[0, CONV_HALO + ts:, :] = next_ref[...]

    off = CONV_HALO - CONV_WIDTH // 2
    rows_read = ts + (off + CONV_WIDTH - 1) // SUBLANES * SUBLANES
    for s in range(1, SUBLANES):
        for c0 in range(0, d, CONV_COPY_COLS):
            for a in range(0, rows_read, CONV_COPY_ROWS):
                n = min(CONV_COPY_ROWS, rows_read - a)
                ubuf[s, a:a + n, c0:c0 + CONV_COPY_COLS] = ubuf[0, a + s:a + s + n, c0:c0 + CONV_COPY_COLS]

    for c0 in range(0, d, cb):
        wt = w_ref[:, c0:c0 + cb]
        bias = b_ref[:, c0:c0 + cb]

        def row_body(rt, carry, c0=c0, wt=wt, bias=bias):
            r0 = pl.multiple_of(rt * rb, rb)
            acc = jnp.zeros((rb, cb), F32) + bias
            for k in range(CONV_WIDTH):
                s, q = (off + k) % SUBLANES, (off + k) // SUBLANES
                acc = acc + ubuf[s, pl.ds(r0 + q * SUBLANES, rb), c0:c0 + cb] * wt[k:k + 1, :]
            cbuf[pl.ds(r0, rb), c0:c0 + cb] = acc
            return carry

        lax.fori_loop(0, ts // rb, row_body, 0)

    v = cbuf[...]
    mu = jnp.mean(v, axis=-1, keepdims=True)
    cen = v - mu
    var = jnp.mean(cen * cen, axis=-1, keepdims=True)
    y = cen * lax.rsqrt(var + EPS) * g_ref[...] + beta_ref[...]
    o_ref[...] = _silu(y).astype(o_ref.dtype)


def conv_ln_swish(u, w_dw, b_dw, ln_g, ln_b, ts=256, rb=32, cb=256):
    b, s, d = u.shape
    hb = ts // CONV_HALO
    n_halo = s // CONV_HALO
    kern = functools.partial(_conv_kernel, ts=ts, rb=rb, cb=cb)
    return pl.pallas_call(
        kern,
        grid=(b, s // ts),
        in_specs=[
            pl.BlockSpec((None, CONV_HALO, d), lambda bi, i: (bi, jnp.maximum(i * hb - 1, 0), 0)),
            pl.BlockSpec((None, ts, d), lambda bi, i: (bi, i, 0)),
            pl.BlockSpec((None, CONV_HALO, d), lambda bi, i: (bi, jnp.minimum((i + 1) * hb, n_halo - 1), 0)),
            pl.BlockSpec((CONV_WIDTH, d), lambda bi, i: (0, 0)),
            pl.BlockSpec((1, d), lambda bi, i: (0, 0)),
            pl.BlockSpec((1, d), lambda bi, i: (0, 0)),
            pl.BlockSpec((1, d), lambda bi, i: (0, 0)),
        ],
        out_specs=pl.BlockSpec((None, ts, d), lambda bi, i: (bi, i, 0)),
        out_shape=jax.ShapeDtypeStruct((b, s, d), BF16),
        scratch_shapes=[pltpu.VMEM((SUBLANES, ts + 2 * CONV_HALO, d), F32), pltpu.VMEM((ts, d), F32)],
        compiler_params=_cparams(("parallel", "arbitrary")),
        name="conv_ln_swish",
    )(u, u, u, w_dw, b_dw.reshape(1, d), ln_g.reshape(1, d), ln_b.reshape(1, d))


def _bf16_bits(v):
    b = pltpu.bitcast(v, jnp.uint32)
    return (b + jnp.uint32(0x7FFF) + ((b >> 16) & jnp.uint32(1))) >> 16


def _router_kernel(x_ref, g_ref, sh_ref, sc_ref, wh_ref, wl_ref, hp_ref, idx_ref, wt_ref, *, n_experts):
    h = _norm_mod(x_ref[...], g_ref[...], sh_ref[...], sc_ref[...])
    half = h.shape[1] // 2
    hp_ref[...] = (_bf16_bits(h[:, half:]) << 16) | _bf16_bits(h[:, :half])
    h_hi = h.astype(BF16)
    h_lo = (h - h_hi.astype(F32)).astype(BF16)
    wh, wl = wh_ref[...], wl_ref[...]
    logits = (jnp.dot(h_hi, wh, preferred_element_type=F32)
              + (jnp.dot(h_hi, wl, preferred_element_type=F32) + jnp.dot(h_lo, wh, preferred_element_type=F32)))
    lane = lax.broadcasted_iota(jnp.int32, logits.shape, 1)
    big = logits.shape[1]
    logits = jnp.where(lane < n_experts, logits, -jnp.inf)
    m1 = jnp.max(logits, axis=-1, keepdims=True)
    i1 = jnp.min(jnp.where(logits == m1, lane, big), axis=-1, keepdims=True)
    rest = jnp.where(lane == i1, -jnp.inf, logits)
    m2 = jnp.max(rest, axis=-1, keepdims=True)
    i2 = jnp.min(jnp.where(rest == m2, lane, big), axis=-1, keepdims=True)
    e = jnp.exp(m2 - m1)
    w1 = 1.0 / (1.0 + e)
    w2 = e / (1.0 + e)
    idx_ref[...] = jnp.where(lane == 0, i1, jnp.where(lane == 1, i2, 0))
    wt_ref[...] = jnp.where(lane == 0, w1, jnp.where(lane == 1, w2, 0.0))


def router(x2d, g, mod, seq, w_router, tm=512):
    r, d = x2d.shape
    n_experts = w_router.shape[1]
    lanes = 128
    wpad = jnp.zeros((d, lanes), F32).at[:, :n_experts].set(w_router)
    wh = wpad.astype(BF16)
    wl = (wpad - wh.astype(F32)).astype(BF16)
    tiles_per_seq = seq // tm
    kern = functools.partial(_router_kernel, n_experts=n_experts)
    hp, idx, wt = pl.pallas_call(
        kern,
        grid=(r // tm,),
        in_specs=[
            pl.BlockSpec((tm, d), lambda i: (i, 0)),
            pl.BlockSpec((1, d), lambda i: (0, 0)),
            pl.BlockSpec((None, 1, d), lambda i: (i // tiles_per_seq, 0, 3)),
            pl.BlockSpec((None, 1, d), lambda i: (i // tiles_per_seq, 0, 4)),
            pl.BlockSpec((d, lanes), lambda i: (0, 0)),
            pl.BlockSpec((d, lanes), lambda i: (0, 0)),
        ],
        out_specs=[
            pl.BlockSpec((tm, d // 2), lambda i: (i, 0)),
            pl.BlockSpec((tm, lanes), lambda i: (i, 0)),
            pl.BlockSpec((tm, lanes), lambda i: (i, 0)),
        ],
        out_shape=[
            jax.ShapeDtypeStruct((r, d // 2), jnp.uint32),
            jax.ShapeDtypeStruct((r, lanes), jnp.int32),
            jax.ShapeDtypeStruct((r, lanes), F32),
        ],
        compiler_params=_cparams(("parallel",)),
        name="router",
    )(x2d, g.reshape(1, d), mod, mod, wh, wl)
    return hp, idx[:, :TOP_K], wt[:, :TOP_K]


def _issue_rows(table_ref, idx_ref, idx_base, dst_of_row, sem, row0, n_rows):
    def body(g, carry):
        base = pl.multiple_of(row0 + g * DMA_UNROLL, DMA_UNROLL)
        for u in range(DMA_UNROLL):
            r = base + u
            pltpu.make_async_copy(table_ref.at[pl.ds(idx_ref[idx_base + r], 1)], dst_of_row(r), sem).start()
        return carry

    lax.fori_loop(0, n_rows // DMA_UNROLL, body, 0)


def _expert_kernel(te_ref, tr_ref, nt_ref, src_ref, hp_ref, wg_ref, wu_ref, wd_ref, o_ref, xbuf, x_scr, xs_scr,
                   sem, *, tm, rows_per_step):
    i = pl.program_id(0)
    f = pl.program_id(1)
    n_live = nt_ref[0]
    live = i < n_live
    slot = i % 2

    @pl.when(jnp.logical_not(live) & (f == 0))
    def _():
        o_ref[...] = jnp.zeros_like(o_ref)

    @pl.when((i == 0) & (f == 0))
    def _():
        _issue_rows(hp_ref, src_ref, 0, lambda r: xbuf.at[0, pl.ds(r, 1)], sem.at[0], 0, tm)

    @pl.when((i + 1 < n_live) & (f < tm // rows_per_step))
    def _():
        _issue_rows(hp_ref, src_ref, (i + 1) * tm, lambda r: xbuf.at[1 - slot, pl.ds(r, 1)], sem.at[1 - slot],
                    f * rows_per_step, rows_per_step)

    @pl.when(live & (f == 0))
    def _():
        pltpu.make_async_copy(hp_ref.at[pl.ds(0, tm)], xbuf.at[slot], sem.at[slot]).wait()
        xp = xbuf[slot]
        lo = pltpu.bitcast(xp << 16, F32)
        hi = pltpu.bitcast(xp & jnp.uint32(0xFFFF0000), F32)
        half = xp.shape[1]
        x_inv = jnp.maximum(jnp.maximum(jnp.max(jnp.abs(lo)), jnp.max(jnp.abs(hi))), F8_TINY) * (1.0 / F8_MAX)
        x_scr[:, :half] = (lo * (1.0 / x_inv)).astype(F8)
        x_scr[:, half:] = (hi * (1.0 / x_inv)).astype(F8)
        xs_scr[0] = x_inv
        o_ref[...] = jnp.zeros_like(o_ref)

    def swiglu_rows(rs, wg, wu, wd):
        (wg8, sg), (wu8, su), (wd8, sd) = wg, wu, wd
        x = x_scr[rs, :]
        x_inv = xs_scr[0]
        gate = jnp.dot(x, wg8, preferred_element_type=F32) * (sg * x_inv)
        up = jnp.dot(x, wu8, preferred_element_type=F32) * (su * x_inv)
        a, a_inv = _quantize(_silu(gate) * up)
        for c0 in range(0, o_ref.shape[1], FFN_OUT_CHUNK):
            cs = slice(c0, c0 + FFN_OUT_CHUNK)
            o_ref[rs, cs] += jnp.dot(a, wd8[:, cs], preferred_element_type=F32) * (sd[:, cs] * a_inv)

    rows_valid = tr_ref[i]
    sparse_rows = EXPERT_SPARSE_BLOCKS * EXPERT_ROW_BLOCK

    @pl.when(live & (rows_valid > sparse_rows))
    def _():
        swiglu_rows(slice(0, tm), _quantize_cols(wg_ref[...]), _quantize_cols(wu_ref[...]),
                    _quantize_cols(wd_ref[...]))

    @pl.when(live & (rows_valid <= sparse_rows))
    def _():
        wg, wu, wd = _quantize_cols(wg_ref[...]), _quantize_cols(wu_ref[...]), _quantize_cols(wd_ref[...])
        for r0 in range(0, sparse_rows, EXPERT_ROW_BLOCK):
            @pl.when(r0 < rows_valid)
            def _(r0=r0):
                swiglu_rows(slice(r0, r0 + EXPERT_ROW_BLOCK), wg, wu, wd)


def expert_ffn(hp, src, tile_expert, tile_rows, n_tiles, wg, wu, wd, tm=1024, tf=256):
    half = hp.shape[1]
    r = src.shape[0]
    d = 2 * half
    dff = wg.shape[2]
    nf = dff // tf
    fetch_steps = next(n for n in range(min(nf, tm // DMA_UNROLL), 0, -1) if (tm // DMA_UNROLL) % n == 0)
    rows_per_step = tm // fetch_steps

    def live_tile(i, nt):
        return jnp.minimum(i, nt[0] - 1)

    def wf(i, f, nt):
        return jnp.where(i < nt[0], f, nf - 1)

    kern = functools.partial(_expert_kernel, tm=tm, rows_per_step=rows_per_step)
    return pl.pallas_call(
        kern,
        grid_spec=pltpu.PrefetchScalarGridSpec(
            num_scalar_prefetch=4,
            grid=(r // tm, nf),
            in_specs=[
                pl.BlockSpec(memory_space=pl.ANY),
                pl.BlockSpec((None, d, tf), lambda i, f, te, tr, nt, src: (te[live_tile(i, nt)], 0, wf(i, f, nt))),
                pl.BlockSpec((None, d, tf), lambda i, f, te, tr, nt, src: (te[live_tile(i, nt)], 0, wf(i, f, nt))),
                pl.BlockSpec((None, tf, d), lambda i, f, te, tr, nt, src: (te[live_tile(i, nt)], wf(i, f, nt), 0)),
            ],
            out_specs=pl.BlockSpec((tm, d), lambda i, f, te, tr, nt, src: (i, 0)),
            scratch_shapes=[pltpu.VMEM((2, tm, half), jnp.uint32), pltpu.VMEM((tm, d), F8),
                            pltpu.SMEM((1,), F32), pltpu.SemaphoreType.DMA((2,))],
        ),
        out_shape=jax.ShapeDtypeStruct((r, d), F32),
        compiler_params=_cparams(("arbitrary", "arbitrary")),
        name="expert_ffn",
    )(tile_expert, tile_rows, n_tiles, src, hp, wg, wu, wd)


def _combine_kernel(pos_ref, y_ref, x_ref, w_ref, gt_ref, fg_ref, o_ref, buf, sem, *, tt):
    i = pl.program_id(0)
    slot = i % 2

    def fetch(tile, dst_slot):
        for k in range(TOP_K):
            _issue_rows(y_ref, pos_ref, tile * tt * TOP_K + k * tt, lambda r, k=k: buf.at[dst_slot, k, pl.ds(r, 1)],
                        sem.at[dst_slot], 0, tt)

    @pl.when(i == 0)
    def _():
        fetch(0, 0)

    @pl.when(i + 1 < pl.num_programs(0))
    def _():
        fetch(i + 1, 1 - slot)

    for k in range(TOP_K):
        pltpu.make_async_copy(y_ref.at[pl.ds(0, tt)], buf.at[slot, k], sem.at[slot]).wait()
    w = w_ref[...]
    y = buf[slot, 0] * w[:, 0:1]
    for k in range(1, TOP_K):
        y = y + buf[slot, k] * w[:, k:k + 1]
    x = x_ref[...] + gt_ref[...] * y
    ms = jnp.mean(x * x, axis=-1, keepdims=True)
    o_ref[...] = x * lax.rsqrt(ms + EPS) * fg_ref[...]


def combine_final(y_sorted, pos_kmajor, top_w, x2d, mod, seq, final_g, tt=256):
    r, d = x2d.shape
    tiles_per_seq = seq // tt
    kern = functools.partial(_combine_kernel, tt=tt)
    return pl.pallas_call(
        kern,
        grid_spec=pltpu.PrefetchScalarGridSpec(
            num_scalar_prefetch=1,
            grid=(r // tt,),
            in_specs=[
                pl.BlockSpec(memory_space=pl.ANY),
                pl.BlockSpec((tt, d), lambda i, pos: (i, 0)),
                pl.BlockSpec((tt, TOP_K), lambda i, pos: (i, 0)),
                pl.BlockSpec((None, 1, d), lambda i, pos: (i // tiles_per_seq, 0, 5)),
                pl.BlockSpec((1, d), lambda i, pos: (0, 0)),
            ],
            out_specs=pl.BlockSpec((tt, d), lambda i, pos: (i, 0)),
            scratch_shapes=[pltpu.VMEM((2, TOP_K, tt, d), F32), pltpu.SemaphoreType.DMA((2,))],
        ),
        out_shape=jax.ShapeDtypeStruct((r, d), F32),
        compiler_params=_cparams(("arbitrary",)),
        name="combine_final",
    )(pos_kmajor, y_sorted, x2d, top_w, mod, final_g.reshape(1, d))


def dispatch_plan(top_i, n_experts, tm, tt):
    t = top_i.shape[0]
    flat_e = top_i.reshape(-1)
    onehot = (flat_e[:, None] == jnp.arange(n_experts)[None, :]).astype(jnp.int32)
    rank = jnp.cumsum(onehot, axis=0) - onehot
    counts = jnp.sum(onehot, axis=0)
    tiles = (counts + tm - 1) // tm
    tile_end = jnp.cumsum(tiles)
    group_start = (tile_end - tiles) * tm
    pos = jnp.sum(onehot * (group_start[None, :] + rank), axis=1)
    n_rows = (t * TOP_K // tm + n_experts) * tm
    src = jnp.zeros((n_rows,), jnp.int32).at[pos].set(jnp.arange(t * TOP_K, dtype=jnp.int32) // TOP_K)
    tile_ids = jnp.arange(n_rows // tm)
    tile_expert = jnp.minimum(jnp.sum(tile_ids[:, None] >= tile_end[None, :], axis=1), n_experts - 1)
    tile_in_group = tile_ids - (tile_end - tiles)[tile_expert]
    tile_rows = jnp.where(tile_ids < tile_end[-1], jnp.clip(counts[tile_expert] - tile_in_group * tm, 0, tm), 0)
    pos_kmajor = pos.astype(jnp.int32).reshape(t // tt, tt, TOP_K).transpose(0, 2, 1).reshape(-1)
    return src, pos_kmajor, tile_expert.astype(jnp.int32), tile_rows.astype(jnp.int32), \
        tile_end[-1:].astype(jnp.int32)


def rope_tables(seq):
    t = np.arange(seq)
    row, col = t // GRID_W, t % GRID_W
    quarter = HEAD_DIM // 4
    inv = ROPE_BASE ** (-np.arange(0, 2 * quarter, 2, dtype=np.float32) / (2 * quarter))
    lane = np.arange(HEAD_DIM)
    pos = np.where(lane[None, :] < HEAD_DIM // 2, row[:, None], col[:, None]).astype(np.float32)
    ang = jnp.asarray(pos) * jnp.asarray(inv[lane % quarter])[None, :]
    cos, sin = jnp.cos(ang), jnp.sin(ang)
    first = jnp.asarray(((lane // quarter) % 2 == 0)[None, :])
    return cos, jnp.where(first, -sin, 0.0), jnp.where(first, 0.0, sin)


def kernel(x, c, ctx, c_ctx, ada_w, ada_b, norm_g, final_g, ab_w_in, ab_rpb, ab_sinks, ab_w_out,
           ffn_w_gate, ffn_w_up, ffn_w_down, conv_w_pw1, conv_b_pw1, conv_w_dw, conv_b_dw,
           conv_ln_g, conv_ln_b, conv_w_pw2, conv_b_pw2, moe_w_router, moe_w_gate, moe_w_up, moe_w_down):
    b, s, d = x.shape
    n_ctx = ctx.shape[1]
    depth = ada_w.shape[0]
    assert depth == 2, "layer schedule below is written for one even and one odd layer"
    na_w = ab_rpb.shape[1] * HEAD_DIM
    sw_q = ab_sinks.shape[1] * HEAD_DIM
    in_w = ab_w_in.shape[2]
    sw_kv = (in_w - 3 * na_w - sw_q) // 2
    n_experts = moe_w_router.shape[2]

    mod_rows = 8
    cvec = jnp.zeros((mod_rows, d), F32).at[:b].set(c).at[b].set(c_ctx)
    mods = ada_mod_all(cvec, ada_w, ada_b).reshape(depth, mod_rows, 1, N_MOD * d)

    x2d = x.reshape(b * s, d)
    ctx2d = ctx.reshape(b * n_ctx, d)

    mod0 = mods[0]
    w_in = ab_w_in[0].astype(BF16)
    tabs = rope_tables(s)
    rope_lo, rope_hi = 3 * na_w, 3 * na_w + sw_q + sw_kv
    tm_in = min(1024, s)
    qkv = in_proj(x2d, norm_g[0, 0], mod0, lambda i: i // (s // tm_in), w_in, tabs, rope_lo, rope_hi, s, tm_in)
    ctx_tabs = tuple(t[:n_ctx] for t in tabs)
    qkv_ctx = in_proj(ctx2d, norm_g[0, 0], mod0, lambda i: b, w_in, ctx_tabs, 0, 0, n_ctx, n_ctx)
    qkv = qkv.reshape(b, s, in_w)
    qkv_ctx = qkv_ctx.reshape(b, n_ctx, in_w)

    na_heads = na_w // HEAD_DIM
    bias = na_bias_table(ab_rpb[0], s // GRID_W)
    y_a = na_attention(qkv, qkv_ctx, bias, na_heads, 0, na_heads, 2 * na_heads)
    y_b = band_attention(qkv, qkv_ctx, ab_sinks[0], sw_q // HEAD_DIM, sw_kv // HEAD_DIM,
                         3 * na_w, 3 * na_w + sw_q, 3 * na_w + sw_q + sw_kv)
    zero_d = jnp.zeros((d,), F32)
    x2d = proj_residual([y_a.reshape(b * s, na_w), y_b.reshape(b * s, sw_q)], ab_w_out[0].astype(BF16),
                        zero_d, x2d, mod0, 2, s)
    x2d = dense_ffn(x2d, norm_g[0, 1], mod0, s, ffn_w_gate[0].astype(BF16), ffn_w_up[0].astype(BF16),
                    ffn_w_down[0].astype(BF16))

    mod1 = mods[1]
    u = pw1_glu(x2d, norm_g[1, 0], mod1, s, conv_w_pw1[0].astype(BF16), conv_b_pw1[0])
    v = conv_ln_swish(u.reshape(b, s, d), conv_w_dw[0], conv_b_dw[0], conv_ln_g[0], conv_ln_b[0])
    x2d = proj_residual([v.reshape(b * s, d)], conv_w_pw2[0].astype(BF16), conv_b_pw2[0], x2d, mod1, 2, s)

    hp, top_i, top_w = router(x2d, norm_g[1, 1], mod1, s, moe_w_router[0])
    tm_e, tt = 1024, 256
    src, pos_kmajor, tile_expert, tile_rows, n_tiles = dispatch_plan(top_i, n_experts, tm_e, tt)
    y_sorted = expert_ffn(hp, src, tile_expert, tile_rows, n_tiles, moe_w_gate[0], moe_w_up[0], moe_w_down[0],
                          tm=tm_e)
    out = combine_final(y_sorted, pos_kmajor, top_w, x2d, mod1, s, final_g, tt=tt)
    return out.reshape(b, s, d)
```

```python
import functools

import numpy as np
import jax
import jax.numpy as jnp
from jax import lax
from jax.experimental import pallas as pl
from jax.experimental.pallas import tpu as pltpu

F32 = jnp.float32
BF16 = jnp.bfloat16
F8 = jnp.float8_e4m3fn
F8_MAX = 448.0
F8_TINY = 1e-30

GRID_W = 64
HEAD_DIM = 128
NA_KH = 8
NA_KW = 16
SW_WINDOW = 128
ROPE_BASE = 10000.0
CONV_WIDTH = 31
N_MOD = 6
TOP_K = 2
EPS = 1e-6
NEG = -1e30

VMEM_LIMIT = 56 * 1024 * 1024
NA_QROWS = 8
CONV_HALO = 16
CONV_COPY_ROWS, CONV_COPY_COLS = 64, 512
FFN_OUT_CHUNK = 512
SUBLANES = 8
EXPERT_ROW_BLOCK = 512
EXPERT_SPARSE_BLOCKS = 3
DMA_UNROLL = 8
ATTN_LOOKAHEAD = 3


def _pick_tile(n, candidates):
    return next(t for t in candidates if n % t == 0)


def _cparams(sem):
    return pltpu.CompilerParams(dimension_semantics=sem, vmem_limit_bytes=VMEM_LIMIT)


def _silu(v):
    return v / (1.0 + jnp.exp(-v))


def _sigmoid(v):
    return 1.0 / (1.0 + jnp.exp(-v))


def _norm_mod(x, g, sh, sc):
    ms = jnp.mean(x * x, axis=-1, keepdims=True)
    y = x * lax.rsqrt(ms + EPS) * g
    return y * (1.0 + sc) + sh


def _ada_kernel(c_ref, w_ref, b_ref, o_ref):
    s = _silu(c_ref[...]).astype(BF16)
    acc = jnp.dot(s, w_ref[...].astype(BF16), preferred_element_type=F32)
    o_ref[...] = acc + b_ref[...]


def ada_mod_all(cvec, ada_w, ada_b, tn=1024):
    depth, d, n = ada_w.shape
    rows = cvec.shape[0]
    return pl.pallas_call(
        _ada_kernel,
        grid=(depth, n // tn),
        in_specs=[
            pl.BlockSpec((rows, d), lambda l, j: (0, 0)),
            pl.BlockSpec((None, d, tn), lambda l, j: (l, 0, j)),
            pl.BlockSpec((None, 1, tn), lambda l, j: (l, 0, j)),
        ],
        out_specs=pl.BlockSpec((None, rows, tn), lambda l, j: (l, 0, j)),
        out_shape=jax.ShapeDtypeStruct((depth, rows, n), F32),
        compiler_params=_cparams(("parallel", "parallel")),
        name="ada_mod",
    )(cvec, ada_w, ada_b.reshape(depth, 1, n))


def _rope_head(v, cos, sa, sb):
    return v * cos + pltpu.roll(v, 96, 1) * sa + pltpu.roll(v, 32, 1) * sb


def _inproj_kernel(x_ref, g_ref, sh_ref, sc_ref, w_ref, cos_ref, sa_ref, sb_ref, o_ref, h_scr,
                   *, rope_lo, rope_hi, tn):
    j = pl.program_id(1)

    @pl.when(j == 0)
    def _():
        h_scr[...] = _norm_mod(x_ref[...], g_ref[...], sh_ref[...], sc_ref[...]).astype(BF16)

    acc = jnp.dot(h_scr[...], w_ref[...], preferred_element_type=F32)
    heads = tn // HEAD_DIM
    first_tile = rope_lo // tn
    last_tile = (rope_hi - 1) // tn if rope_hi > rope_lo else -1

    def plain():
        o_ref[...] = acc.astype(o_ref.dtype)

    if rope_hi <= rope_lo:
        plain()
        return

    pl.when((j < first_tile) | (j > last_tile))(plain)

    for t in range(first_tile, last_tile + 1):
        def roped(t=t):
            cos, sa, sb = cos_ref[...], sa_ref[...], sb_ref[...]
            for hh in range(heads):
                col = t * tn + hh * HEAD_DIM
                v = acc[:, hh * HEAD_DIM:(hh + 1) * HEAD_DIM]
                if rope_lo <= col < rope_hi:
                    v = _rope_head(v, cos, sa, sb)
                o_ref[:, hh * HEAD_DIM:(hh + 1) * HEAD_DIM] = v.astype(o_ref.dtype)
        pl.when(j == t)(roped)


def in_proj(x2d, g, mod, mod_row_of_tile, w_bf16, rope_tabs, rope_lo, rope_hi, seq, tm):
    r, d = x2d.shape
    n = w_bf16.shape[1]
    tn = _pick_tile(n, (512, 256, 128))
    cos, sa, sb = rope_tabs
    tiles_per_seq = seq // tm
    tab_spec = pl.BlockSpec((tm, HEAD_DIM), lambda i, j: (i % tiles_per_seq, 0))
    kern = functools.partial(_inproj_kernel, rope_lo=rope_lo, rope_hi=rope_hi, tn=tn)
    return pl.pallas_call(
        kern,
        grid=(r // tm, n // tn),
        in_specs=[
            pl.BlockSpec((tm, d), lambda i, j: (i, 0)),
            pl.BlockSpec((1, d), lambda i, j: (0, 0)),
            pl.BlockSpec((None, 1, d), lambda i, j: (mod_row_of_tile(i), 0, 0)),
            pl.BlockSpec((None, 1, d), lambda i, j: (mod_row_of_tile(i), 0, 1)),
            pl.BlockSpec((d, tn), lambda i, j: (0, j)),
            tab_spec, tab_spec, tab_spec,
        ],
        out_specs=pl.BlockSpec((tm, tn), lambda i, j: (i, j)),
        out_shape=jax.ShapeDtypeStruct((r, n), BF16),
        scratch_shapes=[pltpu.VMEM((tm, d), BF16)],
        compiler_params=_cparams(("parallel", "arbitrary")),
        name="in_proj",
    )(x2d, g.reshape(1, d), mod, mod, w_bf16, cos, sa, sb)


def _glu_kernel(x_ref, g_ref, sh_ref, sc_ref, wa_ref, wg_ref, ba_ref, bg_ref, o_ref, h_scr):
    j = pl.program_id(1)

    @pl.when(j == 0)
    def _():
        h_scr[...] = _norm_mod(x_ref[...], g_ref[...], sh_ref[...], sc_ref[...]).astype(BF16)

    h = h_scr[...]
    a = jnp.dot(h, wa_ref[...], preferred_element_type=F32) + ba_ref[...]
    gate = jnp.dot(h, wg_ref[...], preferred_element_type=F32) + bg_ref[...]
    o_ref[...] = a * _sigmoid(gate)


def pw1_glu(x2d, g, mod, seq, w_bf16, b, tm=1024, tn=512):
    r, d = x2d.shape
    half = w_bf16.shape[1] // 2
    nj = half // tn
    tiles_per_seq = seq // tm
    b2 = b.reshape(1, 2 * half)
    return pl.pallas_call(
        _glu_kernel,
        grid=(r // tm, nj),
        in_specs=[
            pl.BlockSpec((tm, d), lambda i, j: (i, 0)),
            pl.BlockSpec((1, d), lambda i, j: (0, 0)),
            pl.BlockSpec((None, 1, d), lambda i, j: (i // tiles_per_seq, 0, 0)),
            pl.BlockSpec((None, 1, d), lambda i, j: (i // tiles_per_seq, 0, 1)),
            pl.BlockSpec((d, tn), lambda i, j: (0, j)),
            pl.BlockSpec((d, tn), lambda i, j: (0, j + nj)),
            pl.BlockSpec((1, tn), lambda i, j: (0, j)),
            pl.BlockSpec((1, tn), lambda i, j: (0, j + nj)),
        ],
        out_specs=pl.BlockSpec((tm, tn), lambda i, j: (i, j)),
        out_shape=jax.ShapeDtypeStruct((r, half), F32),
        scratch_shapes=[pltpu.VMEM((tm, d), BF16)],
        compiler_params=_cparams(("parallel", "arbitrary")),
        name="pw1_glu",
    )(x2d, g.reshape(1, d), mod, mod, w_bf16, w_bf16, b2, b2)


def _proj_res_kernel(*refs, n_in, splits):
    y_refs = refs[:n_in]
    w_ref, b_ref, x_ref, gt_ref, o_ref = refs[n_in:]
    acc = None
    for y_ref, (lo, hi) in zip(y_refs, splits):
        part = jnp.dot(y_ref[...], w_ref[lo:hi, :], preferred_element_type=F32)
        acc = part if acc is None else acc + part
    o_ref[...] = x_ref[...] + gt_ref[...] * (acc + b_ref[...])


def proj_residual(ys, w_bf16, b, x2d, mod, gate_chunk, seq, tm=512):
    r, d = x2d.shape
    kdim, n = w_bf16.shape
    tn = n
    tiles_per_seq = seq // tm
    splits, lo = [], 0
    for y in ys:
        splits.append((lo, lo + y.shape[1]))
        lo += y.shape[1]
    assert lo == kdim
    chunks_per_d = d // tn
    kern = functools.partial(_proj_res_kernel, n_in=len(ys), splits=tuple(splits))
    return pl.pallas_call(
        kern,
        grid=(r // tm, n // tn),
        in_specs=[pl.BlockSpec((tm, y.shape[1]), lambda i, j: (i, 0)) for y in ys] + [
            pl.BlockSpec((kdim, tn), lambda i, j: (0, j)),
            pl.BlockSpec((1, tn), lambda i, j: (0, j)),
            pl.BlockSpec((tm, tn), lambda i, j: (i, j)),
            pl.BlockSpec((None, 1, tn),
                         lambda i, j: (i // tiles_per_seq, 0, gate_chunk * chunks_per_d + j)),
        ],
        out_specs=pl.BlockSpec((tm, tn), lambda i, j: (i, j)),
        out_shape=jax.ShapeDtypeStruct((r, n), F32),
        compiler_params=_cparams(("parallel", "parallel")),
        name="proj_residual",
    )(*ys, w_bf16, b.reshape(1, n), x2d, mod)


def _na_kernel(q_ref, k_ref, v_ref, kc_ref, vc_ref, bias_ref, o_ref, *, scale, rows, kh):
    i = pl.program_id(2)
    nt = (((1,), (1,)), ((), ()))
    kc, vc = kc_ref[...], vc_ref[...]
    s_ctx = lax.dot_general(q_ref[...], kc, nt, preferred_element_type=F32) * scale
    def scores(rq):
        r = i * NA_QROWS + rq
        rs = jnp.clip(r - kh // 2, 0, rows - kh)
        start = pl.multiple_of(rs * GRID_W, GRID_W)
        qs = slice(rq * GRID_W, (rq + 1) * GRID_W)
        kw = k_ref[pl.ds(start, kh * GRID_W), :]
        s = lax.dot_general(q_ref[qs, :], kw, nt, preferred_element_type=F32) * scale + bias_ref[r - rs]
        return s, start

    def finish(rq, s, start):
        qs = slice(rq * GRID_W, (rq + 1) * GRID_W)
        vw = v_ref[pl.ds(start, kh * GRID_W), :]
        sc = s_ctx[qs, :]
        m = jnp.maximum(jnp.max(s, axis=-1, keepdims=True), jnp.max(sc, axis=-1, keepdims=True))
        p = jnp.exp(s - m)
        pc = jnp.exp(sc - m)
        l = jnp.sum(p, axis=-1, keepdims=True) + jnp.sum(pc, axis=-1, keepdims=True)
        o = (jnp.dot(p.astype(BF16), vw, preferred_element_type=F32)
             + jnp.dot(pc.astype(BF16), vc, preferred_element_type=F32))
        o_ref[qs, :] = (o / l).astype(o_ref.dtype)

    pending = [scores(rq) for rq in range(ATTN_LOOKAHEAD)]
    for rq in range(NA_QROWS):
        if rq + ATTN_LOOKAHEAD < NA_QROWS:
            pending.append(scores(rq + ATTN_LOOKAHEAD))
        finish(rq, *pending.pop(0))


def na_bias_table(rpb, rows):
    heads = rpb.shape[0]
    kh = min(NA_KH, rows)
    c = np.arange(GRID_W)
    dc = c[None, :] - c[:, None]
    col_start = np.clip(c - NA_KW // 2, 0, GRID_W - NA_KW)
    col_valid = (c[None, :] >= col_start[:, None]) & (c[None, :] < col_start[:, None] + NA_KW)
    col_idx = np.clip(dc + NA_KW - 1, 0, 2 * NA_KW - 2)
    toep = jnp.take(rpb, jnp.asarray(col_idx.reshape(-1)), axis=2).reshape(heads, 2 * NA_KH - 1, GRID_W, GRID_W)
    toep = jnp.where(jnp.asarray(col_valid)[None, None], toep, NEG).transpose(0, 2, 1, 3)
    tables = []
    for delta in range(kh):
        lo = NA_KH - 1 - delta
        tables.append(toep[:, :, lo:lo + kh, :].reshape(heads, GRID_W, kh * GRID_W))
    return jnp.stack(tables, axis=1)


def na_attention(qkv, qkv_ctx, bias, heads, q_blk0, k_blk0, v_blk0):
    b, s, _ = qkv.shape
    n_ctx = qkv_ctx.shape[1]
    rows = s // GRID_W
    kh = min(NA_KH, rows)
    tq = NA_QROWS * GRID_W
    kern = functools.partial(_na_kernel, scale=HEAD_DIM ** -0.5, rows=rows, kh=kh)
    return pl.pallas_call(
        kern,
        grid=(b, heads, rows // NA_QROWS),
        in_specs=[
            pl.BlockSpec((None, tq, HEAD_DIM), lambda bi, h, i: (bi, i, q_blk0 + h)),
            pl.BlockSpec((None, s, HEAD_DIM), lambda bi, h, i: (bi, 0, k_blk0 + h)),
            pl.BlockSpec((None, s, HEAD_DIM), lambda bi, h, i: (bi, 0, v_blk0 + h)),
            pl.BlockSpec((None, n_ctx, HEAD_DIM), lambda bi, h, i: (bi, 0, k_blk0 + h)),
            pl.BlockSpec((None, n_ctx, HEAD_DIM), lambda bi, h, i: (bi, 0, v_blk0 + h)),
            pl.BlockSpec((None, kh, GRID_W, kh * GRID_W), lambda bi, h, i: (h, 0, 0, 0)),
        ],
        out_specs=pl.BlockSpec((None, tq, HEAD_DIM), lambda bi, h, i: (bi, i, h)),
        out_shape=jax.ShapeDtypeStruct((b, s, heads * HEAD_DIM), BF16),
        compiler_params=_cparams(("parallel", "parallel", "arbitrary")),
        name="na_attention",
    )(qkv, qkv, qkv, qkv_ctx, qkv_ctx, bias)


def _band_kernel(q_ref, k_ref, v_ref, kc_ref, vc_ref, sink_ref, o_ref, kcat, vcat, *, scale, blk, group, seq):
    kvh = pl.program_id(1)
    n = pl.program_id(2)
    n_win = 3 * blk
    start = pl.multiple_of(jnp.clip((n - 1) * blk, 0, seq - n_win), blk)
    kcat[0:n_win, :] = k_ref[pl.ds(start, n_win), :]
    vcat[0:n_win, :] = v_ref[pl.ds(start, n_win), :]
    kcat[n_win:, :] = kc_ref[...]
    vcat[n_win:, :] = vc_ref[...]
    nk = kcat.shape[0]
    qpos = n * blk + lax.broadcasted_iota(jnp.int32, (blk, nk), 0)
    col = lax.broadcasted_iota(jnp.int32, (blk, nk), 1)
    kpos = start + col
    valid = (col >= n_win) | (jnp.abs(qpos - kpos) <= SW_WINDOW)
    kk, vv = kcat[...], vcat[...]
    def scores(g):
        hs = slice(g * HEAD_DIM, (g + 1) * HEAD_DIM)
        return lax.dot_general(q_ref[:, hs], kk, (((1,), (1,)), ((), ())), preferred_element_type=F32) * scale

    def finish(g, s):
        s = jnp.where(valid, s, NEG)
        sink = sink_ref[kvh * group + g]
        m = jnp.maximum(jnp.max(s, axis=-1, keepdims=True), sink)
        p = jnp.exp(s - m)
        l = jnp.sum(p, axis=-1, keepdims=True) + jnp.exp(sink - m)
        o = jnp.dot(p.astype(BF16), vv, preferred_element_type=F32) / l
        o_ref[:, g * HEAD_DIM:(g + 1) * HEAD_DIM] = o.astype(o_ref.dtype)

    pending = [scores(g) for g in range(ATTN_LOOKAHEAD)]
    for g in range(group):
        if g + ATTN_LOOKAHEAD < group:
            pending.append(scores(g + ATTN_LOOKAHEAD))
        finish(g, pending.pop(0))


def band_attention(qkv, qkv_ctx, sinks, q_heads, kv_heads, q_col0, k_col0, v_col0, blk=128):
    b, s, _ = qkv.shape
    n_ctx = qkv_ctx.shape[1]
    group = q_heads // kv_heads
    gw = group * HEAD_DIM
    nk = 3 * blk + n_ctx
    kern = functools.partial(_band_kernel, scale=HEAD_DIM ** -0.5, blk=blk, group=group, seq=s)
    return pl.pallas_call(
        kern,
        grid=(b, kv_heads, s // blk),
        in_specs=[
            pl.BlockSpec((None, blk, gw), lambda bi, kv, n: (bi, n, q_col0 // gw + kv)),
            pl.BlockSpec((None, s, HEAD_DIM), lambda bi, kv, n: (bi, 0, k_col0 // HEAD_DIM + kv)),
            pl.BlockSpec((None, s, HEAD_DIM), lambda bi, kv, n: (bi, 0, v_col0 // HEAD_DIM + kv)),
            pl.BlockSpec((None, n_ctx, HEAD_DIM), lambda bi, kv, n: (bi, 0, k_col0 // HEAD_DIM + kv)),
            pl.BlockSpec((None, n_ctx, HEAD_DIM), lambda bi, kv, n: (bi, 0, v_col0 // HEAD_DIM + kv)),
            pl.BlockSpec(memory_space=pltpu.SMEM),
        ],
        out_specs=pl.BlockSpec((None, blk, gw), lambda bi, kv, n: (bi, n, kv)),
        out_shape=jax.ShapeDtypeStruct((b, s, q_heads * HEAD_DIM), BF16),
        scratch_shapes=[pltpu.VMEM((nk, HEAD_DIM), BF16), pltpu.VMEM((nk, HEAD_DIM), BF16)],
        compiler_params=_cparams(("parallel", "parallel", "arbitrary")),
        name="band_attention",
    )(qkv, qkv, qkv, qkv_ctx, qkv_ctx, sinks)


def _quantize(v):
    amax = jnp.maximum(jnp.max(jnp.abs(v)), F8_TINY)
    return (v * (F8_MAX / amax)).astype(F8), amax * (1.0 / F8_MAX)


def _quantize_cols(w):
    amax = jnp.maximum(jnp.max(jnp.abs(w), axis=0, keepdims=True), F8_TINY)
    return (w * (F8_MAX / amax)).astype(F8), amax * (1.0 / F8_MAX)


def _ffn_kernel(x_ref, g_ref, sh_ref, sc_ref, gt_ref, wg_ref, wu_ref, wd_ref, o_ref, h_scr):
    f = pl.program_id(1)

    @pl.when(f == 0)
    def _():
        h_scr[...] = _norm_mod(x_ref[...], g_ref[...], sh_ref[...], sc_ref[...]).astype(BF16)
        o_ref[...] = jnp.zeros_like(o_ref)

    h = h_scr[...]
    gate = jnp.dot(h, wg_ref[...], preferred_element_type=F32)
    up = jnp.dot(h, wu_ref[...], preferred_element_type=F32)
    a = (_silu(gate) * up).astype(BF16)
    for c0 in range(0, o_ref.shape[1], FFN_OUT_CHUNK):
        cs = slice(c0, c0 + FFN_OUT_CHUNK)
        o_ref[:, cs] += jnp.dot(a, wd_ref[:, cs], preferred_element_type=F32)

    @pl.when(f == pl.num_programs(1) - 1)
    def _():
        o_ref[...] = x_ref[...] + gt_ref[...] * o_ref[...]


def dense_ffn(x2d, g, mod, seq, wg, wu, wd, tm=1024, tf=512):
    r, d = x2d.shape
    dff = wg.shape[1]
    tiles_per_seq = seq // tm
    row = lambda i, f: i // tiles_per_seq
    return pl.pallas_call(
        _ffn_kernel,
        grid=(r // tm, dff // tf),
        in_specs=[
            pl.BlockSpec((tm, d), lambda i, f: (i, 0), pipeline_mode=pl.Buffered(1)),
            pl.BlockSpec((1, d), lambda i, f: (0, 0)),
            pl.BlockSpec((None, 1, d), lambda i, f: (row(i, f), 0, 3)),
            pl.BlockSpec((None, 1, d), lambda i, f: (row(i, f), 0, 4)),
            pl.BlockSpec((None, 1, d), lambda i, f: (row(i, f), 0, 5)),
            pl.BlockSpec((d, tf), lambda i, f: (0, f)),
            pl.BlockSpec((d, tf), lambda i, f: (0, f)),
            pl.BlockSpec((tf, d), lambda i, f: (f, 0)),
        ],
        out_specs=pl.BlockSpec((tm, d), lambda i, f: (i, 0)),
        out_shape=jax.ShapeDtypeStruct((r, d), F32),
        scratch_shapes=[pltpu.VMEM((tm, d), BF16)],
        compiler_params=_cparams(("parallel", "arbitrary")),
        name="dense_ffn",
    )(x2d, g.reshape(1, d), mod, mod, mod, wg, wu, wd)


def _conv_kernel(prev_ref, cur_ref, next_ref, w_ref, b_ref, g_ref, beta_ref, o_ref, ubuf, cbuf, *, ts, rb, cb):
    i = pl.program_id(1)
    last = pl.num_programs(1) - 1
    d = cur_ref.shape[1]
    zeros = jnp.zeros((CONV_HALO, d), F32)

    @pl.when(i == 0)
    def _():
        ubuf[0, 0:CONV_HALO, :] = zeros

    @pl.when(i > 0)
    def _():
        ubuf[0, 0:CONV_HALO, :] = prev_ref[...]

    ubuf[0, CONV_HALO:CONV_HALO + ts, :] = cur_ref[...]

    @pl.when(i == last)
    def _():
        ubuf[0, CONV_HALO + ts:, :] = zeros

    @pl.when(i < last)
    def _():
        ubuf[0, CONV_HALO + ts:, :] = next_ref[...]

    off = CONV_HALO - CONV_WIDTH // 2
    rows_read = ts + (off + CONV_WIDTH - 1) // SUBLANES * SUBLANES
    for s in range(1, SUBLANES):
        for c0 in range(0, d, CONV_COPY_COLS):
            for a in range(0, rows_read, CONV_COPY_ROWS):
                n = min(CONV_COPY_ROWS, rows_read - a)
                ubuf[s, a:a + n, c0:c0 + CONV_COPY_COLS] = ubuf[0, a + s:a + s + n, c0:c0 + CONV_COPY_COLS]

    for c0 in range(0, d, cb):
        wt = w_ref[:, c0:c0 + cb]
        bias = b_ref[:, c0:c0 + cb]

        def row_body(rt, carry, c0=c0, wt=wt, bias=bias):
            r0 = pl.multiple_of(rt * rb, rb)
            acc = jnp.zeros((rb, cb), F32) + bias
            for k in range(CONV_WIDTH):
                s, q = (off + k) % SUBLANES, (off + k) // SUBLANES
                acc = acc + ubuf[s, pl.ds(r0 + q * SUBLANES, rb), c0:c0 + cb] * wt[k:k + 1, :]
            cbuf[pl.ds(r0, rb), c0:c0 + cb] = acc
            return carry

        lax.fori_loop(0, ts // rb, row_body, 0)

    v = cbuf[...]
    mu = jnp.mean(v, axis=-1, keepdims=True)
    cen = v - mu
    var = jnp.mean(cen * cen, axis=-1, keepdims=True)
    y = cen * lax.rsqrt(var + EPS) * g_ref[...] + beta_ref[...]
    o_ref[...] = _silu(y).astype(o_ref.dtype)


def conv_ln_swish(u, w_dw, b_dw, ln_g, ln_b, ts=256, rb=32, cb=256):
    b, s, d = u.shape
    hb = ts // CONV_HALO
    n_halo = s // CONV_HALO
    kern = functools.partial(_conv_kernel, ts=ts, rb=rb, cb=cb)
    return pl.pallas_call(
        kern,
        grid=(b, s // ts),
        in_specs=[
            pl.BlockSpec((None, CONV_HALO, d), lambda bi, i: (bi, jnp.maximum(i * hb - 1, 0), 0)),
            pl.BlockSpec((None, ts, d), lambda bi, i: (bi, i, 0)),
            pl.BlockSpec((None, CONV_HALO, d), lambda bi, i: (bi, jnp.minimum((i + 1) * hb, n_halo - 1), 0)),
            pl.BlockSpec((CONV_WIDTH, d), lambda bi, i: (0, 0)),
            pl.BlockSpec((1, d), lambda bi, i: (0, 0)),
            pl.BlockSpec((1, d), lambda bi, i: (0, 0)),
            pl.BlockSpec((1, d), lambda bi, i: (0, 0)),
        ],
        out_specs=pl.BlockSpec((None, ts, d), lambda bi, i: (bi, i, 0)),
        out_shape=jax.ShapeDtypeStruct((b, s, d), BF16),
        scratch_shapes=[pltpu.VMEM((SUBLANES, ts + 2 * CONV_HALO, d), F32), pltpu.VMEM((ts, d), F32)],
        compiler_params=_cparams(("parallel", "arbitrary")),
        name="conv_ln_swish",
    )(u, u, u, w_dw, b_dw.reshape(1, d), ln_g.reshape(1, d), ln_b.reshape(1, d))


def _bf16_bits(v):
    b = pltpu.bitcast(v, jnp.uint32)
    return (b + jnp.uint32(0x7FFF) + ((b >> 16) & jnp.uint32(1))) >> 16


def _router_kernel(x_ref, g_ref, sh_ref, sc_ref, wh_ref, wl_ref, hp_ref, idx_ref, wt_ref, *, n_experts):
    h = _norm_mod(x_ref[...], g_ref[...], sh_ref[...], sc_ref[...])
    half = h.shape[1] // 2
    hp_ref[...] = (_bf16_bits(h[:, half:]) << 16) | _bf16_bits(h[:, :half])
    h_hi = h.astype(BF16)
    h_lo = (h - h_hi.astype(F32)).astype(BF16)
    wh, wl = wh_ref[...], wl_ref[...]
    logits = (jnp.dot(h_hi, wh, preferred_element_type=F32)
              + (jnp.dot(h_hi, wl, preferred_element_type=F32) + jnp.dot(h_lo, wh, preferred_element_type=F32)))
    lane = lax.broadcasted_iota(jnp.int32, logits.shape, 1)
    big = logits.shape[1]
    logits = jnp.where(lane < n_experts, logits, -jnp.inf)
    m1 = jnp.max(logits, axis=-1, keepdims=True)
    i1 = jnp.min(jnp.where(logits == m1, lane, big), axis=-1, keepdims=True)
    rest = jnp.where(lane == i1, -jnp.inf, logits)
    m2 = jnp.max(rest, axis=-1, keepdims=True)
    i2 = jnp.min(jnp.where(rest == m2, lane, big), axis=-1, keepdims=True)
    e = jnp.exp(m2 - m1)
    w1 = 1.0 / (1.0 + e)
    w2 = e / (1.0 + e)
    idx_ref[...] = jnp.where(lane == 0, i1, jnp.where(lane == 1, i2, 0))
    wt_ref[...] = jnp.where(lane == 0, w1, jnp.where(lane == 1, w2, 0.0))


def router(x2d, g, mod, seq, w_router, tm=512):
    r, d = x2d.shape
    n_experts = w_router.shape[1]
    lanes = 128
    wpad = jnp.zeros((d, lanes), F32).at[:, :n_experts].set(w_router)
    wh = wpad.astype(BF16)
    wl = (wpad - wh.astype(F32)).astype(BF16)
    tiles_per_seq = seq // tm
    kern = functools.partial(_router_kernel, n_experts=n_experts)
    hp, idx, wt = pl.pallas_call(
        kern,
        grid=(r // tm,),
        in_specs=[
            pl.BlockSpec((tm, d), lambda i: (i, 0)),
            pl.BlockSpec((1, d), lambda i: (0, 0)),
            pl.BlockSpec((None, 1, d), lambda i: (i // tiles_per_seq, 0, 3)),
            pl.BlockSpec((None, 1, d), lambda i: (i // tiles_per_seq, 0, 4)),
            pl.BlockSpec((d, lanes), lambda i: (0, 0)),
            pl.BlockSpec((d, lanes), lambda i: (0, 0)),
        ],
        out_specs=[
            pl.BlockSpec((tm, d // 2), lambda i: (i, 0)),
            pl.BlockSpec((tm, lanes), lambda i: (i, 0)),
            pl.BlockSpec((tm, lanes), lambda i: (i, 0)),
        ],
        out_shape=[
            jax.ShapeDtypeStruct((r, d // 2), jnp.uint32),
            jax.ShapeDtypeStruct((r, lanes), jnp.int32),
            jax.ShapeDtypeStruct((r, lanes), F32),
        ],
        compiler_params=_cparams(("parallel",)),
        name="router",
    )(x2d, g.reshape(1, d), mod, mod, wh, wl)
    return hp, idx[:, :TOP_K], wt[:, :TOP_K]


def _issue_rows(table_ref, idx_ref, idx_base, dst_of_row, sem, row0, n_rows):
    def body(g, carry):
        base = pl.multiple_of(row0 + g * DMA_UNROLL, DMA_UNROLL)
        for u in range(DMA_UNROLL):
            r = base + u
            pltpu.make_async_copy(table_ref.at[pl.ds(idx_ref[idx_base + r], 1)], dst_of_row(r), sem).start()
        return carry

    lax.fori_loop(0, n_rows // DMA_UNROLL, body, 0)


def _expert_kernel(te_ref, tr_ref, nt_ref, src_ref, hp_ref, wg_ref, wu_ref, wd_ref, o_ref, xbuf, x_scr, xs_scr,
                   sem, *, tm, rows_per_step):
    i = pl.program_id(0)
    f = pl.program_id(1)
    n_live = nt_ref[0]
    live = i < n_live

    @pl.when(jnp.logical_not(live) & (f == 0))
    def _():
        o_ref[...] = jnp.zeros_like(o_ref)

    @pl.when((i == 0) & (f == 0))
    def _():
        _issue_rows(hp_ref, src_ref, 0, lambda r: xbuf.at[pl.ds(r, 1)], sem, 0, tm)

    @pl.when(live & (f == 0))
    def _():
        pltpu.make_async_copy(hp_ref.at[pl.ds(0, tm)], xbuf, sem).wait()
        xp = xbuf[...]
        lo = pltpu.bitcast(xp << 16, F32)
        hi = pltpu.bitcast(xp & jnp.uint32(0xFFFF0000), F32)
        half = xp.shape[1]
        x_inv = jnp.maximum(jnp.maximum(jnp.max(jnp.abs(lo)), jnp.max(jnp.abs(hi))), F8_TINY) * (1.0 / F8_MAX)
        x_scr[:, :half] = (lo * (1.0 / x_inv)).astype(F8)
        x_scr[:, half:] = (hi * (1.0 / x_inv)).astype(F8)
        xs_scr[0] = x_inv
        o_ref[...] = jnp.zeros_like(o_ref)

    @pl.when((i + 1 < n_live) & (f < tm // rows_per_step))
    def _():
        _issue_rows(hp_ref, src_ref, (i + 1) * tm, lambda r: xbuf.at[pl.ds(r, 1)], sem, f * rows_per_step,
                    rows_per_step)

    def swiglu_rows(rs, wg, wu, wd):
        (wg8, sg), (wu8, su), (wd8, sd) = wg, wu, wd
        x = x_scr[rs, :]
        x_inv = xs_scr[0]
        gate = jnp.dot(x, wg8, preferred_element_type=F32) * (sg * x_inv)
        up = jnp.dot(x, wu8, preferred_element_type=F32) * (su * x_inv)
        a, a_inv = _quantize(_silu(gate) * up)
        for c0 in range(0, o_ref.shape[1], FFN_OUT_CHUNK):
            cs = slice(c0, c0 + FFN_OUT_CHUNK)
            o_ref[rs, cs] += jnp.dot(a, wd8[:, cs], preferred_element_type=F32) * (sd[:, cs] * a_inv)

    rows_valid = tr_ref[i]
    sparse_rows = EXPERT_SPARSE_BLOCKS * EXPERT_ROW_BLOCK

    @pl.when(live & (rows_valid > sparse_rows))
    def _():
        swiglu_rows(slice(0, tm), _quantize_cols(wg_ref[...]), _quantize_cols(wu_ref[...]),
                    _quantize_cols(wd_ref[...]))

    @pl.when(live & (rows_valid <= sparse_rows))
    def _():
        wg, wu, wd = _quantize_cols(wg_ref[...]), _quantize_cols(wu_ref[...]), _quantize_cols(wd_ref[...])
        for r0 in range(0, sparse_rows, EXPERT_ROW_BLOCK):
            @pl.when(r0 < rows_valid)
            def _(r0=r0):
                swiglu_rows(slice(r0, r0 + EXPERT_ROW_BLOCK), wg, wu, wd)


def expert_ffn(hp, src, tile_expert, tile_rows, n_tiles, wg, wu, wd, tm, tf=256):
    half = hp.shape[1]
    r = src.shape[0]
    d = 2 * half
    dff = wg.shape[2]
    nf = dff // tf
    fetch_steps = next(n for n in range(min(nf, tm // DMA_UNROLL), 0, -1) if (tm // DMA_UNROLL) % n == 0)
    rows_per_step = tm // fetch_steps

    def live_tile(i, nt):
        return jnp.minimum(i, nt[0] - 1)

    def wf(i, f, nt):
        return jnp.where(i < nt[0], f, nf - 1)

    kern = functools.partial(_expert_kernel, tm=tm, rows_per_step=rows_per_step)
    return pl.pallas_call(
        kern,
        grid_spec=pltpu.PrefetchScalarGridSpec(
            num_scalar_prefetch=4,
            grid=(r // tm, nf),
            in_specs=[
                pl.BlockSpec(memory_space=pl.ANY),
                pl.BlockSpec((None, d, tf), lambda i, f, te, tr, nt, src: (te[live_tile(i, nt)], 0, wf(i, f, nt))),
                pl.BlockSpec((None, d, tf), lambda i, f, te, tr, nt, src: (te[live_tile(i, nt)], 0, wf(i, f, nt))),
                pl.BlockSpec((None, tf, d), lambda i, f, te, tr, nt, src: (te[live_tile(i, nt)], wf(i, f, nt), 0)),
            ],
            out_specs=pl.BlockSpec((tm, d), lambda i, f, te, tr, nt, src: (i, 0), pipeline_mode=pl.Buffered(1)),
            scratch_shapes=[pltpu.VMEM((tm, half), jnp.uint32), pltpu.VMEM((tm, d), F8),
                            pltpu.SMEM((1,), F32), pltpu.SemaphoreType.DMA(())],
        ),
        out_shape=jax.ShapeDtypeStruct((r, d), F32),
        compiler_params=_cparams(("arbitrary", "arbitrary")),
        name="expert_ffn",
    )(tile_expert, tile_rows, n_tiles, src, hp, wg, wu, wd)


def _combine_kernel(pos_ref, y_ref, x_ref, w_ref, gt_ref, fg_ref, o_ref, buf, sem, *, tt):
    i = pl.program_id(0)
    slot = i % 2

    def fetch(tile, dst_slot):
        for k in range(TOP_K):
            _issue_rows(y_ref, pos_ref, tile * tt * TOP_K + k * tt, lambda r, k=k: buf.at[dst_slot, k, pl.ds(r, 1)],
                        sem.at[dst_slot], 0, tt)

    @pl.when(i == 0)
    def _():
        fetch(0, 0)

    @pl.when(i + 1 < pl.num_programs(0))
    def _():
        fetch(i + 1, 1 - slot)

    for k in range(TOP_K):
        pltpu.make_async_copy(y_ref.at[pl.ds(0, tt)], buf.at[slot, k], sem.at[slot]).wait()
    w = w_ref[...]
    y = buf[slot, 0] * w[:, 0:1]
    for k in range(1, TOP_K):
        y = y + buf[slot, k] * w[:, k:k + 1]
    x = x_ref[...] + gt_ref[...] * y
    ms = jnp.mean(x * x, axis=-1, keepdims=True)
    o_ref[...] = x * lax.rsqrt(ms + EPS) * fg_ref[...]


def combine_final(y_sorted, pos_kmajor, top_w, x2d, mod, seq, final_g, tt=256):
    r, d = x2d.shape
    tiles_per_seq = seq // tt
    kern = functools.partial(_combine_kernel, tt=tt)
    return pl.pallas_call(
        kern,
        grid_spec=pltpu.PrefetchScalarGridSpec(
            num_scalar_prefetch=1,
            grid=(r // tt,),
            in_specs=[
                pl.BlockSpec(memory_space=pl.ANY),
                pl.BlockSpec((tt, d), lambda i, pos: (i, 0)),
                pl.BlockSpec((tt, TOP_K), lambda i, pos: (i, 0)),
                pl.BlockSpec((None, 1, d), lambda i, pos: (i // tiles_per_seq, 0, 5)),
                pl.BlockSpec((1, d), lambda i, pos: (0, 0)),
            ],
            out_specs=pl.BlockSpec((tt, d), lambda i, pos: (i, 0)),
            scratch_shapes=[pltpu.VMEM((2, TOP_K, tt, d), F32), pltpu.SemaphoreType.DMA((2,))],
        ),
        out_shape=jax.ShapeDtypeStruct((r, d), F32),
        compiler_params=_cparams(("arbitrary",)),
        name="combine_final",
    )(pos_kmajor, y_sorted, x2d, top_w, mod, final_g.reshape(1, d))


def dispatch_plan(top_i, n_experts, tm, tt):
    t = top_i.shape[0]
    flat_e = top_i.reshape(-1)
    onehot = (flat_e[:, None] == jnp.arange(n_experts)[None, :]).astype(jnp.int32)
    rank = jnp.cumsum(onehot, axis=0) - onehot
    counts = jnp.sum(onehot, axis=0)
    tiles = (counts + tm - 1) // tm
    tile_end = jnp.cumsum(tiles)
    group_start = (tile_end - tiles) * tm
    pos = jnp.sum(onehot * (group_start[None, :] + rank), axis=1)
    n_rows = (t * TOP_K // tm + n_experts) * tm
    src = jnp.zeros((n_rows,), jnp.int32).at[pos].set(jnp.arange(t * TOP_K, dtype=jnp.int32) // TOP_K)
    tile_ids = jnp.arange(n_rows // tm)
    tile_expert = jnp.minimum(jnp.sum(tile_ids[:, None] >= tile_end[None, :], axis=1), n_experts - 1)
    tile_in_group = tile_ids - (tile_end - tiles)[tile_expert]
    tile_rows = jnp.where(tile_ids < tile_end[-1], jnp.clip(counts[tile_expert] - tile_in_group * tm, 0, tm), 0)
    pos_kmajor = pos.astype(jnp.int32).reshape(t // tt, tt, TOP_K).transpose(0, 2, 1).reshape(-1)
    return src, pos_kmajor, tile_expert.astype(jnp.int32), tile_rows.astype(jnp.int32), \
        tile_end[-1:].astype(jnp.int32)


def rope_tables(seq):
    t = np.arange(seq)
    row, col = t // GRID_W, t % GRID_W
    quarter = HEAD_DIM // 4
    inv = ROPE_BASE ** (-np.arange(0, 2 * quarter, 2, dtype=np.float32) / (2 * quarter))
    lane = np.arange(HEAD_DIM)
    pos = np.where(lane[None, :] < HEAD_DIM // 2, row[:, None], col[:, None]).astype(np.float32)
    ang = jnp.asarray(pos) * jnp.asarray(inv[lane % quarter])[None, :]
    cos, sin = jnp.cos(ang), jnp.sin(ang)
    first = jnp.asarray(((lane // quarter) % 2 == 0)[None, :])
    return cos, jnp.where(first, -sin, 0.0), jnp.where(first, 0.0, sin)


def kernel(x, c, ctx, c_ctx, ada_w, ada_b, norm_g, final_g, ab_w_in, ab_rpb, ab_sinks, ab_w_out,
           ffn_w_gate, ffn_w_up, ffn_w_down, conv_w_pw1, conv_b_pw1, conv_w_dw, conv_b_dw,
           conv_ln_g, conv_ln_b, conv_w_pw2, conv_b_pw2, moe_w_router, moe_w_gate, moe_w_up, moe_w_down):
    b, s, d = x.shape
    n_ctx = ctx.shape[1]
    depth = ada_w.shape[0]
    assert depth == 2, "layer schedule below is written for one even and one odd layer"
    na_w = ab_rpb.shape[1] * HEAD_DIM
    sw_q = ab_sinks.shape[1] * HEAD_DIM
    in_w = ab_w_in.shape[2]
    sw_kv = (in_w - 3 * na_w - sw_q) // 2
    n_experts = moe_w_router.shape[2]

    mod_rows = 8
    cvec = jnp.zeros((mod_rows, d), F32).at[:b].set(c).at[b].set(c_ctx)
    mods = ada_mod_all(cvec, ada_w, ada_b).reshape(depth, mod_rows, 1, N_MOD * d)

    x2d = x.reshape(b * s, d)
    ctx2d = ctx.reshape(b * n_ctx, d)

    mod0 = mods[0]
    w_in = ab_w_in[0].astype(BF16)
    tabs = rope_tables(s)
    rope_lo, rope_hi = 3 * na_w, 3 * na_w + sw_q + sw_kv
    tm_in = min(1024, s)
    qkv = in_proj(x2d, norm_g[0, 0], mod0, lambda i: i // (s // tm_in), w_in, tabs, rope_lo, rope_hi, s, tm_in)
    ctx_tabs = tuple(t[:n_ctx] for t in tabs)
    qkv_ctx = in_proj(ctx2d, norm_g[0, 0], mod0, lambda i: b, w_in, ctx_tabs, 0, 0, n_ctx, n_ctx)
    qkv = qkv.reshape(b, s, in_w)
    qkv_ctx = qkv_ctx.reshape(b, n_ctx, in_w)

    na_heads = na_w // HEAD_DIM
    bias = na_bias_table(ab_rpb[0], s // GRID_W)
    y_a = na_attention(qkv, qkv_ctx, bias, na_heads, 0, na_heads, 2 * na_heads)
    y_b = band_attention(qkv, qkv_ctx, ab_sinks[0], sw_q // HEAD_DIM, sw_kv // HEAD_DIM,
                         3 * na_w, 3 * na_w + sw_q, 3 * na_w + sw_q + sw_kv)
    zero_d = jnp.zeros((d,), F32)
    x2d = proj_residual([y_a.reshape(b * s, na_w), y_b.reshape(b * s, sw_q)], ab_w_out[0].astype(BF16),
                        zero_d, x2d, mod0, 2, s)
    x2d = dense_ffn(x2d, norm_g[0, 1], mod0, s, ffn_w_gate[0].astype(BF16), ffn_w_up[0].astype(BF16),
                    ffn_w_down[0].astype(BF16))

    mod1 = mods[1]
    u = pw1_glu(x2d, norm_g[1, 0], mod1, s, conv_w_pw1[0].astype(BF16), conv_b_pw1[0])
    v = conv_ln_swish(u.reshape(b, s, d), conv_w_dw[0], conv_b_dw[0], conv_ln_g[0], conv_ln_b[0])
    x2d = proj_residual([v.reshape(b * s, d)], conv_w_pw2[0].astype(BF16), conv_b_pw2[0], x2d, mod1, 2, s)

    hp, top_i, top_w = router(x2d, norm_g[1, 1], mod1, s, moe_w_router[0])
    tm_e, tt = 2048, 256
    src, pos_kmajor, tile_expert, tile_rows, n_tiles = dispatch_plan(top_i, n_experts, tm_e, tt)
    y_sorted = expert_ffn(hp, src, tile_expert, tile_rows, n_tiles, moe_w_gate[0], moe_w_up[0], moe_w_down[0],
                          tm=tm_e)
    out = combine_final(y_sorted, pos_kmajor, top_w, x2d, mod1, s, final_g, tt=tt)
    return out.reshape(b, s, d)
```

```python
import functools

import numpy as np
import jax
import jax.numpy as jnp
from jax import lax
from jax.experimental import pallas as pl
from jax.experimental.pallas import tpu as pltpu

F32 = jnp.float32
BF16 = jnp.bfloat16
F8 = jnp.float8_e4m3fn
F8_MAX = 448.0
F8_TINY = 1e-30

GRID_W = 64
HEAD_DIM = 128
NA_KH = 8
NA_KW = 16
SW_WINDOW = 128
ROPE_BASE = 10000.0
CONV_WIDTH = 31
N_MOD = 6
TOP_K = 2
EPS = 1e-6
NEG = -1e30

VMEM_LIMIT = 56 * 1024 * 1024
NA_QROWS = 8
CONV_HALO = 16
CONV_COPY_ROWS, CONV_COPY_COLS = 64, 512
FFN_OUT_CHUNK = 512
SUBLANES = 8
EXPERT_ROW_BLOCK = 256
EXPERT_SPARSE_BLOCKS = 2
DMA_UNROLL = 16
ATTN_LOOKAHEAD = 3
NA_LOOKAHEAD = 5


def _pick_tile(n, candidates):
    return next(t for t in candidates if n % t == 0)


def _cparams(sem):
    return pltpu.CompilerParams(dimension_semantics=sem, vmem_limit_bytes=VMEM_LIMIT)


def _silu(v):
    return v / (1.0 + jnp.exp(-v))


def _sigmoid(v):
    return 1.0 / (1.0 + jnp.exp(-v))


def _norm_mod(x, g, sh, sc):
    ms = jnp.mean(x * x, axis=-1, keepdims=True)
    y = x * lax.rsqrt(ms + EPS) * g
    return y * (1.0 + sc) + sh


def _ada_kernel(c_ref, w_ref, b_ref, o_ref):
    s = _silu(c_ref[...]).astype(BF16)
    acc = jnp.dot(s, w_ref[...].astype(BF16), preferred_element_type=F32)
    o_ref[...] = acc + b_ref[...]


def ada_mod_all(cvec, ada_w, ada_b, tn=1024):
    depth, d, n = ada_w.shape
    rows = cvec.shape[0]
    return pl.pallas_call(
        _ada_kernel,
        grid=(depth, n // tn),
        in_specs=[
            pl.BlockSpec((rows, d), lambda l, j: (0, 0)),
            pl.BlockSpec((None, d, tn), lambda l, j: (l, 0, j)),
            pl.BlockSpec((None, 1, tn), lambda l, j: (l, 0, j)),
        ],
        out_specs=pl.BlockSpec((None, rows, tn), lambda l, j: (l, 0, j)),
        out_shape=jax.ShapeDtypeStruct((depth, rows, n), F32),
        compiler_params=_cparams(("parallel", "parallel")),
        name="ada_mod",
    )(cvec, ada_w, ada_b.reshape(depth, 1, n))


def _rope_head(v, cos, sn):
    return v * cos + pltpu.roll(v, HEAD_DIM // 2, 1) * sn


def _inproj_kernel(x_ref, g_ref, sh_ref, sc_ref, w_ref, cos_ref, sn_ref, o_ref, h_scr, *, rope_lo, rope_hi, tn):
    j = pl.program_id(1)

    @pl.when(j == 0)
    def _():
        h_scr[...] = _norm_mod(x_ref[...], g_ref[...], sh_ref[...], sc_ref[...]).astype(BF16)

    acc = jnp.dot(h_scr[...], w_ref[...], preferred_element_type=F32)
    heads = tn // HEAD_DIM
    first_tile = rope_lo // tn
    last_tile = (rope_hi - 1) // tn if rope_hi > rope_lo else -1

    def plain():
        o_ref[...] = acc.astype(o_ref.dtype)

    if rope_hi <= rope_lo:
        plain()
        return

    pl.when((j < first_tile) | (j > last_tile))(plain)

    for t in range(first_tile, last_tile + 1):
        def roped(t=t):
            cos, sn = cos_ref[...], sn_ref[...]
            for hh in range(heads):
                col = t * tn + hh * HEAD_DIM
                v = acc[:, hh * HEAD_DIM:(hh + 1) * HEAD_DIM]
                if rope_lo <= col < rope_hi:
                    v = _rope_head(v, cos, sn)
                o_ref[:, hh * HEAD_DIM:(hh + 1) * HEAD_DIM] = v.astype(o_ref.dtype)
        pl.when(j == t)(roped)


def in_proj(x2d, g, mod, mod_row_of_tile, w_bf16, rope_tabs, rope_lo, rope_hi, seq, tm):
    r, d = x2d.shape
    n = w_bf16.shape[1]
    tn = _pick_tile(n, (512, 256, 128))
    cos, sn = rope_tabs
    tiles_per_seq = seq // tm
    tab_spec = pl.BlockSpec((tm, HEAD_DIM), lambda i, j: (i % tiles_per_seq, 0))
    kern = functools.partial(_inproj_kernel, rope_lo=rope_lo, rope_hi=rope_hi, tn=tn)
    return pl.pallas_call(
        kern,
        grid=(r // tm, n // tn),
        in_specs=[
            pl.BlockSpec((tm, d), lambda i, j: (i, 0)),
            pl.BlockSpec((1, d), lambda i, j: (0, 0)),
            pl.BlockSpec((None, 1, d), lambda i, j: (mod_row_of_tile(i), 0, 0)),
            pl.BlockSpec((None, 1, d), lambda i, j: (mod_row_of_tile(i), 0, 1)),
            pl.BlockSpec((d, tn), lambda i, j: (0, j)),
            tab_spec, tab_spec,
        ],
        out_specs=pl.BlockSpec((tm, tn), lambda i, j: (i, j)),
        out_shape=jax.ShapeDtypeStruct((r, n), BF16),
        scratch_shapes=[pltpu.VMEM((tm, d), BF16)],
        compiler_params=_cparams(("parallel", "arbitrary")),
        name="in_proj",
    )(x2d, g.reshape(1, d), mod, mod, w_bf16, cos, sn)


def _glu_kernel(x_ref, g_ref, sh_ref, sc_ref, wa_ref, wg_ref, ba_ref, bg_ref, o_ref, h_scr):
    j = pl.program_id(1)

    @pl.when(j == 0)
    def _():
        h_scr[...] = _norm_mod(x_ref[...], g_ref[...], sh_ref[...], sc_ref[...]).astype(BF16)

    h = h_scr[...]
    a = jnp.dot(h, wa_ref[...], preferred_element_type=F32) + ba_ref[...]
    gate = jnp.dot(h, wg_ref[...], preferred_element_type=F32) + bg_ref[...]
    o_ref[...] = a * _sigmoid(gate)


def pw1_glu(x2d, g, mod, seq, w_bf16, b, tm=1024, tn=512):
    r, d = x2d.shape
    half = w_bf16.shape[1] // 2
    nj = half // tn
    tiles_per_seq = seq // tm
    b2 = b.reshape(1, 2 * half)
    return pl.pallas_call(
        _glu_kernel,
        grid=(r // tm, nj),
        in_specs=[
            pl.BlockSpec((tm, d), lambda i, j: (i, 0)),
            pl.BlockSpec((1, d), lambda i, j: (0, 0)),
            pl.BlockSpec((None, 1, d), lambda i, j: (i // tiles_per_seq, 0, 0)),
            pl.BlockSpec((None, 1, d), lambda i, j: (i // tiles_per_seq, 0, 1)),
            pl.BlockSpec((d, tn), lambda i, j: (0, j)),
            pl.BlockSpec((d, tn), lambda i, j: (0, j + nj)),
            pl.BlockSpec((1, tn), lambda i, j: (0, j)),
            pl.BlockSpec((1, tn), lambda i, j: (0, j + nj)),
        ],
        out_specs=pl.BlockSpec((tm, tn), lambda i, j: (i, j)),
        out_shape=jax.ShapeDtypeStruct((r, half), F32),
        scratch_shapes=[pltpu.VMEM((tm, d), BF16)],
        compiler_params=_cparams(("parallel", "arbitrary")),
        name="pw1_glu",
    )(x2d, g.reshape(1, d), mod, mod, w_bf16, w_bf16, b2, b2)


def _proj_res_kernel(*refs, n_in, splits):
    y_refs = refs[:n_in]
    w_ref, b_ref, x_ref, gt_ref, o_ref = refs[n_in:]
    acc = None
    for y_ref, (lo, hi) in zip(y_refs, splits):
        part = jnp.dot(y_ref[...], w_ref[lo:hi, :], preferred_element_type=F32)
        acc = part if acc is None else acc + part
    o_ref[...] = x_ref[...] + gt_ref[...] * (acc + b_ref[...])


def proj_residual(ys, w_bf16, b, x2d, mod, gate_chunk, seq, tm=512):
    r, d = x2d.shape
    kdim, n = w_bf16.shape
    tn = n
    tiles_per_seq = seq // tm
    splits, lo = [], 0
    for y in ys:
        splits.append((lo, lo + y.shape[1]))
        lo += y.shape[1]
    assert lo == kdim
    chunks_per_d = d // tn
    kern = functools.partial(_proj_res_kernel, n_in=len(ys), splits=tuple(splits))
    return pl.pallas_call(
        kern,
        grid=(r // tm, n // tn),
        in_specs=[pl.BlockSpec((tm, y.shape[1]), lambda i, j: (i, 0)) for y in ys] + [
            pl.BlockSpec((kdim, tn), lambda i, j: (0, j)),
            pl.BlockSpec((1, tn), lambda i, j: (0, j)),
            pl.BlockSpec((tm, tn), lambda i, j: (i, j)),
            pl.BlockSpec((None, 1, tn),
                         lambda i, j: (i // tiles_per_seq, 0, gate_chunk * chunks_per_d + j)),
        ],
        out_specs=pl.BlockSpec((tm, tn), lambda i, j: (i, j)),
        out_shape=jax.ShapeDtypeStruct((r, n), F32),
        compiler_params=_cparams(("parallel", "parallel")),
        name="proj_residual",
    )(*ys, w_bf16, b.reshape(1, n), x2d, mod)


def _na_kernel(q_ref, k_ref, v_ref, kc_ref, vc_ref, bias_ref, o_ref, *, scale, rows, kh):
    i = pl.program_id(2)
    nt = (((1,), (1,)), ((), ()))
    kc, vc = kc_ref[...], vc_ref[...]
    s_ctx = lax.dot_general(q_ref[...], kc, nt, preferred_element_type=F32) * scale
    def scores(rq):
        r = i * NA_QROWS + rq
        rs = jnp.clip(r - kh // 2, 0, rows - kh)
        start = pl.multiple_of(rs * GRID_W, GRID_W)
        qs = slice(rq * GRID_W, (rq + 1) * GRID_W)
        kw = k_ref[pl.ds(start, kh * GRID_W), :]
        s = lax.dot_general(q_ref[qs, :], kw, nt, preferred_element_type=F32) * scale + bias_ref[r - rs]
        return s, start

    def finish(rq, s, start):
        qs = slice(rq * GRID_W, (rq + 1) * GRID_W)
        vw = v_ref[pl.ds(start, kh * GRID_W), :]
        sc = s_ctx[qs, :]
        m = jnp.maximum(jnp.max(s, axis=-1, keepdims=True), jnp.max(sc, axis=-1, keepdims=True))
        p = jnp.exp(s - m)
        pc = jnp.exp(sc - m)
        l = jnp.sum(p, axis=-1, keepdims=True) + jnp.sum(pc, axis=-1, keepdims=True)
        o = (jnp.dot(p.astype(BF16), vw, preferred_element_type=F32)
             + jnp.dot(pc.astype(BF16), vc, preferred_element_type=F32))
        o_ref[qs, :] = (o / l).astype(o_ref.dtype)

    pending = [scores(rq) for rq in range(NA_LOOKAHEAD)]
    for rq in range(NA_QROWS):
        if rq + NA_LOOKAHEAD < NA_QROWS:
            pending.append(scores(rq + NA_LOOKAHEAD))
        finish(rq, *pending.pop(0))


def na_bias_table(rpb, rows):
    heads = rpb.shape[0]
    kh = min(NA_KH, rows)
    c = np.arange(GRID_W)
    dc = c[None, :] - c[:, None]
    col_start = np.clip(c - NA_KW // 2, 0, GRID_W - NA_KW)
    col_valid = (c[None, :] >= col_start[:, None]) & (c[None, :] < col_start[:, None] + NA_KW)
    col_idx = np.clip(dc + NA_KW - 1, 0, 2 * NA_KW - 2)
    toep = jnp.take(rpb, jnp.asarray(col_idx.reshape(-1)), axis=2).reshape(heads, 2 * NA_KH - 1, GRID_W, GRID_W)
    toep = jnp.where(jnp.asarray(col_valid)[None, None], toep, NEG).transpose(0, 2, 1, 3)
    tables = []
    for delta in range(kh):
        lo = NA_KH - 1 - delta
        tables.append(toep[:, :, lo:lo + kh, :].reshape(heads, GRID_W, kh * GRID_W))
    return jnp.stack(tables, axis=1)


def na_attention(qkv, qkv_ctx, bias, heads, q_blk0, k_blk0, v_blk0):
    b, s, _ = qkv.shape
    n_ctx = qkv_ctx.shape[1]
    rows = s // GRID_W
    kh = min(NA_KH, rows)
    tq = NA_QROWS * GRID_W
    kern = functools.partial(_na_kernel, scale=HEAD_DIM ** -0.5, rows=rows, kh=kh)
    return pl.pallas_call(
        kern,
        grid=(b, heads, rows // NA_QROWS),
        in_specs=[
            pl.BlockSpec((None, tq, HEAD_DIM), lambda bi, h, i: (bi, i, q_blk0 + h)),
            pl.BlockSpec((None, s, HEAD_DIM), lambda bi, h, i: (bi, 0, k_blk0 + h)),
            pl.BlockSpec((None, s, HEAD_DIM), lambda bi, h, i: (bi, 0, v_blk0 + h)),
            pl.BlockSpec((None, n_ctx, HEAD_DIM), lambda bi, h, i: (bi, 0, k_blk0 + h)),
            pl.BlockSpec((None, n_ctx, HEAD_DIM), lambda bi, h, i: (bi, 0, v_blk0 + h)),
            pl.BlockSpec((None, kh, GRID_W, kh * GRID_W), lambda bi, h, i: (h, 0, 0, 0)),
        ],
        out_specs=pl.BlockSpec((None, tq, HEAD_DIM), lambda bi, h, i: (bi, i, h)),
        out_shape=jax.ShapeDtypeStruct((b, s, heads * HEAD_DIM), BF16),
        compiler_params=_cparams(("parallel", "parallel", "arbitrary")),
        name="na_attention",
    )(qkv, qkv, qkv, qkv_ctx, qkv_ctx, bias)


def _band_kernel(q_ref, k_ref, v_ref, kc_ref, vc_ref, sink_ref, o_ref, kcat, vcat, *, scale, blk, group, seq):
    kvh = pl.program_id(1)
    n = pl.program_id(2)
    n_win = 3 * blk
    start = pl.multiple_of(jnp.clip((n - 1) * blk, 0, seq - n_win), blk)
    kcat[0:n_win, :] = k_ref[pl.ds(start, n_win), :]
    vcat[0:n_win, :] = v_ref[pl.ds(start, n_win), :]
    kcat[n_win:, :] = kc_ref[...]
    vcat[n_win:, :] = vc_ref[...]
    nk = kcat.shape[0]
    qpos = n * blk + lax.broadcasted_iota(jnp.int32, (blk, nk), 0)
    col = lax.broadcasted_iota(jnp.int32, (blk, nk), 1)
    kpos = start + col
    valid = (col >= n_win) | (jnp.abs(qpos - kpos) <= SW_WINDOW)
    kk, vv = kcat[...], vcat[...]
    def scores(g):
        hs = slice(g * HEAD_DIM, (g + 1) * HEAD_DIM)
        return lax.dot_general(q_ref[:, hs], kk, (((1,), (1,)), ((), ())), preferred_element_type=F32) * scale

    def finish(g, s):
        s = jnp.where(valid, s, NEG)
        sink = sink_ref[kvh * group + g]
        m = jnp.maximum(jnp.max(s, axis=-1, keepdims=True), sink)
        p = jnp.exp(s - m)
        l = jnp.sum(p, axis=-1, keepdims=True) + jnp.exp(sink - m)
        o = jnp.dot(p.astype(BF16), vv, preferred_element_type=F32) / l
        o_ref[:, g * HEAD_DIM:(g + 1) * HEAD_DIM] = o.astype(o_ref.dtype)

    pending = [scores(g) for g in range(ATTN_LOOKAHEAD)]
    for g in range(group):
        if g + ATTN_LOOKAHEAD < group:
            pending.append(scores(g + ATTN_LOOKAHEAD))
        finish(g, pending.pop(0))


def band_attention(qkv, qkv_ctx, sinks, q_heads, kv_heads, q_col0, k_col0, v_col0, blk=128):
    b, s, _ = qkv.shape
    n_ctx = qkv_ctx.shape[1]
    group = q_heads // kv_heads
    gw = group * HEAD_DIM
    nk = 3 * blk + n_ctx
    kern = functools.partial(_band_kernel, scale=HEAD_DIM ** -0.5, blk=blk, group=group, seq=s)
    return pl.pallas_call(
        kern,
        grid=(b, kv_heads, s // blk),
        in_specs=[
            pl.BlockSpec((None, blk, gw), lambda bi, kv, n: (bi, n, q_col0 // gw + kv)),
            pl.BlockSpec((None, s, HEAD_DIM), lambda bi, kv, n: (bi, 0, k_col0 // HEAD_DIM + kv)),
            pl.BlockSpec((None, s, HEAD_DIM), lambda bi, kv, n: (bi, 0, v_col0 // HEAD_DIM + kv)),
            pl.BlockSpec((None, n_ctx, HEAD_DIM), lambda bi, kv, n: (bi, 0, k_col0 // HEAD_DIM + kv)),
            pl.BlockSpec((None, n_ctx, HEAD_DIM), lambda bi, kv, n: (bi, 0, v_col0 // HEAD_DIM + kv)),
            pl.BlockSpec(memory_space=pltpu.SMEM),
        ],
        out_specs=pl.BlockSpec((None, blk, gw), lambda bi, kv, n: (bi, n, kv)),
        out_shape=jax.ShapeDtypeStruct((b, s, q_heads * HEAD_DIM), BF16),
        scratch_shapes=[pltpu.VMEM((nk, HEAD_DIM), BF16), pltpu.VMEM((nk, HEAD_DIM), BF16)],
        compiler_params=_cparams(("parallel", "parallel", "arbitrary")),
        name="band_attention",
    )(qkv, qkv, qkv, qkv_ctx, qkv_ctx, sinks)


def _quantize(v):
    amax = jnp.maximum(jnp.max(jnp.abs(v)), F8_TINY)
    return (v * (F8_MAX / amax)).astype(F8), amax * (1.0 / F8_MAX)


def _quantize_cols(w):
    amax = jnp.maximum(jnp.max(jnp.abs(w), axis=0, keepdims=True), F8_TINY)
    return (w * (F8_MAX / amax)).astype(F8), amax * (1.0 / F8_MAX)


def _ffn_kernel(x_ref, g_ref, sh_ref, sc_ref, gt_ref, wg_ref, wu_ref, wd_ref, o_ref, h_scr):
    f = pl.program_id(1)

    @pl.when(f == 0)
    def _():
        h_scr[...] = _norm_mod(x_ref[...], g_ref[...], sh_ref[...], sc_ref[...]).astype(BF16)
        o_ref[...] = jnp.zeros_like(o_ref)

    h = h_scr[...]
    gate = jnp.dot(h, wg_ref[...], preferred_element_type=F32)
    up = jnp.dot(h, wu_ref[...], preferred_element_type=F32)
    a = (_silu(gate) * up).astype(BF16)
    for c0 in range(0, o_ref.shape[1], FFN_OUT_CHUNK):
        cs = slice(c0, c0 + FFN_OUT_CHUNK)
        o_ref[:, cs] += jnp.dot(a, wd_ref[:, cs], preferred_element_type=F32)

    @pl.when(f == pl.num_programs(1) - 1)
    def _():
        o_ref[...] = x_ref[...] + gt_ref[...] * o_ref[...]


def dense_ffn(x2d, g, mod, seq, wg, wu, wd, tm=1024, tf=512):
    r, d = x2d.shape
    dff = wg.shape[1]
    tiles_per_seq = seq // tm
    row = lambda i, f: i // tiles_per_seq
    return pl.pallas_call(
        _ffn_kernel,
        grid=(r // tm, dff // tf),
        in_specs=[
            pl.BlockSpec((tm, d), lambda i, f: (i, 0), pipeline_mode=pl.Buffered(1)),
            pl.BlockSpec((1, d), lambda i, f: (0, 0)),
            pl.BlockSpec((None, 1, d), lambda i, f: (row(i, f), 0, 3)),
            pl.BlockSpec((None, 1, d), lambda i, f: (row(i, f), 0, 4)),
            pl.BlockSpec((None, 1, d), lambda i, f: (row(i, f), 0, 5)),
            pl.BlockSpec((d, tf), lambda i, f: (0, f)),
            pl.BlockSpec((d, tf), lambda i, f: (0, f)),
            pl.BlockSpec((tf, d), lambda i, f: (f, 0)),
        ],
        out_specs=pl.BlockSpec((tm, d), lambda i, f: (i, 0)),
        out_shape=jax.ShapeDtypeStruct((r, d), F32),
        scratch_shapes=[pltpu.VMEM((tm, d), BF16)],
        compiler_params=_cparams(("parallel", "arbitrary")),
        name="dense_ffn",
    )(x2d, g.reshape(1, d), mod, mod, mod, wg, wu, wd)


def _conv_kernel(prev_ref, cur_ref, next_ref, w_ref, b_ref, g_ref, beta_ref, o_ref, ubuf, cbuf, *, ts, rb, cb):
    i = pl.program_id(1)
    last = pl.num_programs(1) - 1
    d = cur_ref.shape[1]
    zeros = jnp.zeros((CONV_HALO, d), F32)

    @pl.when(i == 0)
    def _():
        ubuf[0, 0:CONV_HALO, :] = zeros

    @pl.when(i > 0)
    def _():
        ubuf[0, 0:CONV_HALO, :] = prev_ref[...]

    ubuf[0, CONV_HALO:CONV_HALO + ts, :] = cur_ref[...]

    @pl.when(i == last)
    def _():
        ubuf[0, CONV_HALO + ts:, :] = zeros

    @pl.when(i < last)
    def _():
        ubuf[0, CONV_HALO + ts:, :] = next_ref[...]

    off = CONV_HALO - CONV_WIDTH // 2
    rows_read = ts + (off + CONV_WIDTH - 1) // SUBLANES * SUBLANES
    for s in range(1, SUBLANES):
        for c0 in range(0, d, CONV_COPY_COLS):
            for a in range(0, rows_read, CONV_COPY_ROWS):
                n = min(CONV_COPY_ROWS, rows_read - a)
                ubuf[s, a:a + n, c0:c0 + CONV_COPY_COLS] = ubuf[0, a + s:a + s + n, c0:c0 + CONV_COPY_COLS]

    for c0 in range(0, d, cb):
        wt = w_ref[:, c0:c0 + cb]
        bias = b_ref[:, c0:c0 + cb]

        def row_body(rt, carry, c0=c0, wt=wt, bias=bias):
            r0 = pl.multiple_of(rt * rb, rb)
            acc = jnp.zeros((rb, cb), F32) + bias
            for k in range(CONV_WIDTH):
                s, q = (off + k) % SUBLANES, (off + k) // SUBLANES
                acc = acc + ubuf[s, pl.ds(r0 + q * SUBLANES, rb), c0:c0 + cb] * wt[k:k + 1, :]
            cbuf[pl.ds(r0, rb), c0:c0 + cb] = acc
            return carry

        lax.fori_loop(0, ts // rb, row_body, 0)

    v = cbuf[...]
    mu = jnp.mean(v, axis=-1, keepdims=True)
    cen = v - mu
    var = jnp.mean(cen * cen, axis=-1, keepdims=True)
    y = cen * lax.rsqrt(var + EPS) * g_ref[...] + beta_ref[...]
    o_ref[...] = _silu(y).astype(o_ref.dtype)


def conv_ln_swish(u, w_dw, b_dw, ln_g, ln_b, ts=256, rb=32, cb=256):
    b, s, d = u.shape
    hb = ts // CONV_HALO
    n_halo = s // CONV_HALO
    kern = functools.partial(_conv_kernel, ts=ts, rb=rb, cb=cb)
    return pl.pallas_call(
        kern,
        grid=(b, s // ts),
        in_specs=[
            pl.BlockSpec((None, CONV_HALO, d), lambda bi, i: (bi, jnp.maximum(i * hb - 1, 0), 0)),
            pl.BlockSpec((None, ts, d), lambda bi, i: (bi, i, 0)),
            pl.BlockSpec((None, CONV_HALO, d), lambda bi, i: (bi, jnp.minimum((i + 1) * hb, n_halo - 1), 0)),
            pl.BlockSpec((CONV_WIDTH, d), lambda bi, i: (0, 0)),
            pl.BlockSpec((1, d), lambda bi, i: (0, 0)),
            pl.BlockSpec((1, d), lambda bi, i: (0, 0)),
            pl.BlockSpec((1, d), lambda bi, i: (0, 0)),
        ],
        out_specs=pl.BlockSpec((None, ts, d), lambda bi, i: (bi, i, 0)),
        out_shape=jax.ShapeDtypeStruct((b, s, d), BF16),
        scratch_shapes=[pltpu.VMEM((SUBLANES, ts + 2 * CONV_HALO, d), F32), pltpu.VMEM((ts, d), F32)],
        compiler_params=_cparams(("parallel", "arbitrary")),
        name="conv_ln_swish",
    )(u, u, u, w_dw, b_dw.reshape(1, d), ln_g.reshape(1, d), ln_b.reshape(1, d))


def _bf16_bits(v):
    b = pltpu.bitcast(v, jnp.uint32)
    return (b + jnp.uint32(0x7FFF) + ((b >> 16) & jnp.uint32(1))) >> 16


def _router_kernel(x_ref, g_ref, sh_ref, sc_ref, wh_ref, wl_ref, hp_ref, idx_ref, wt_ref, *, n_experts):
    h = _norm_mod(x_ref[...], g_ref[...], sh_ref[...], sc_ref[...])
    half = h.shape[1] // 2
    hp_ref[...] = (_bf16_bits(h[:, half:]) << 16) | _bf16_bits(h[:, :half])
    h_hi = h.astype(BF16)
    h_lo = (h - h_hi.astype(F32)).astype(BF16)
    wh, wl = wh_ref[...], wl_ref[...]
    logits = (jnp.dot(h_hi, wh, preferred_element_type=F32)
              + (jnp.dot(h_hi, wl, preferred_element_type=F32) + jnp.dot(h_lo, wh, preferred_element_type=F32)))
    lane = lax.broadcasted_iota(jnp.int32, logits.shape, 1)
    big = logits.shape[1]
    logits = jnp.where(lane < n_experts, logits, -jnp.inf)
    m1 = jnp.max(logits, axis=-1, keepdims=True)
    i1 = jnp.min(jnp.where(logits == m1, lane, big), axis=-1, keepdims=True)
    rest = jnp.where(lane == i1, -jnp.inf, logits)
    m2 = jnp.max(rest, axis=-1, keepdims=True)
    i2 = jnp.min(jnp.where(rest == m2, lane, big), axis=-1, keepdims=True)
    e = jnp.exp(m2 - m1)
    w1 = 1.0 / (1.0 + e)
    w2 = e / (1.0 + e)
    idx_ref[...] = jnp.where(lane == 0, i1, jnp.where(lane == 1, i2, 0))
    wt_ref[...] = jnp.where(lane == 0, w1, jnp.where(lane == 1, w2, 0.0))


def router(x2d, g, mod, seq, w_router, tm=512):
    r, d = x2d.shape
    n_experts = w_router.shape[1]
    lanes = 128
    wpad = jnp.zeros((d, lanes), F32).at[:, :n_experts].set(w_router)
    wh = wpad.astype(BF16)
    wl = (wpad - wh.astype(F32)).astype(BF16)
    tiles_per_seq = seq // tm
    kern = functools.partial(_router_kernel, n_experts=n_experts)
    hp, idx, wt = pl.pallas_call(
        kern,
        grid=(r // tm,),
        in_specs=[
            pl.BlockSpec((tm, d), lambda i: (i, 0)),
            pl.BlockSpec((1, d), lambda i: (0, 0)),
            pl.BlockSpec((None, 1, d), lambda i: (i // tiles_per_seq, 0, 3)),
            pl.BlockSpec((None, 1, d), lambda i: (i // tiles_per_seq, 0, 4)),
            pl.BlockSpec((d, lanes), lambda i: (0, 0)),
            pl.BlockSpec((d, lanes), lambda i: (0, 0)),
        ],
        out_specs=[
            pl.BlockSpec((tm, d // 2), lambda i: (i, 0)),
            pl.BlockSpec((tm, lanes), lambda i: (i, 0)),
            pl.BlockSpec((tm, lanes), lambda i: (i, 0)),
        ],
        out_shape=[
            jax.ShapeDtypeStruct((r, d // 2), jnp.uint32),
            jax.ShapeDtypeStruct((r, lanes), jnp.int32),
            jax.ShapeDtypeStruct((r, lanes), F32),
        ],
        compiler_params=_cparams(("parallel",)),
        name="router",
    )(x2d, g.reshape(1, d), mod, mod, wh, wl)
    return hp, idx[:, :TOP_K], wt[:, :TOP_K]


def _issue_rows(table_ref, idx_ref, idx_base, dst_of_row, sem, row0, n_rows):
    def body(g, carry):
        base = pl.multiple_of(row0 + g * DMA_UNROLL, DMA_UNROLL)
        for u in range(DMA_UNROLL):
            r = base + u
            pltpu.make_async_copy(table_ref.at[pl.ds(idx_ref[idx_base + r], 1)], dst_of_row(r), sem).start()
        return carry

    lax.fori_loop(0, n_rows // DMA_UNROLL, body, 0)


def _expert_kernel(te_ref, tr_ref, nt_ref, src_ref, hp_ref, wg_ref, wu_ref, wd_ref, o_ref, xbuf, x_scr, xs_scr,
                   sem, *, tm, rows_per_step):
    i = pl.program_id(0)
    f = pl.program_id(1)
    n_live = nt_ref[0]
    live = i < n_live
    slot = i % 2

    @pl.when(jnp.logical_not(live) & (f == 0))
    def _():
        o_ref[...] = jnp.zeros_like(o_ref)

    @pl.when((i == 0) & (f == 0))
    def _():
        _issue_rows(hp_ref, src_ref, 0, lambda r: xbuf.at[0, pl.ds(r, 1)], sem.at[0], 0, tm)

    @pl.when((i + 1 < n_live) & (f < tm // rows_per_step))
    def _():
        _issue_rows(hp_ref, src_ref, (i + 1) * tm, lambda r: xbuf.at[1 - slot, pl.ds(r, 1)], sem.at[1 - slot],
                    f * rows_per_step, rows_per_step)

    @pl.when(live & (f == 0))
    def _():
        pltpu.make_async_copy(hp_ref.at[pl.ds(0, tm)], xbuf.at[slot], sem.at[slot]).wait()
        xp = xbuf[slot]
        lo = pltpu.bitcast(xp << 16, F32)
        hi = pltpu.bitcast(xp & jnp.uint32(0xFFFF0000), F32)
        half = xp.shape[1]
        x_inv = jnp.maximum(jnp.maximum(jnp.max(jnp.abs(lo)), jnp.max(jnp.abs(hi))), F8_TINY) * (1.0 / F8_MAX)
        x_scr[:, :half] = (lo * (1.0 / x_inv)).astype(F8)
        x_scr[:, half:] = (hi * (1.0 / x_inv)).astype(F8)
        xs_scr[0] = x_inv
        o_ref[...] = jnp.zeros_like(o_ref)

    def swiglu_rows(rs, wg, wu, wd):
        (wg8, sg), (wu8, su), (wd8, sd) = wg, wu, wd
        x = x_scr[rs, :]
        x_inv = xs_scr[0]
        gate = jnp.dot(x, wg8, preferred_element_type=F32) * (sg * x_inv)
        up = jnp.dot(x, wu8, preferred_element_type=F32) * (su * x_inv)
        a, a_inv = _quantize(_silu(gate) * up)
        for c0 in range(0, o_ref.shape[1], FFN_OUT_CHUNK):
            cs = slice(c0, c0 + FFN_OUT_CHUNK)
            o_ref[rs, cs] += jnp.dot(a, wd8[:, cs], preferred_element_type=F32) * (sd[:, cs] * a_inv)

    rows_valid = tr_ref[i]
    sparse_rows = EXPERT_SPARSE_BLOCKS * EXPERT_ROW_BLOCK

    @pl.when(live & (rows_valid > sparse_rows))
    def _():
        swiglu_rows(slice(0, tm), _quantize_cols(wg_ref[...]), _quantize_cols(wu_ref[...]),
                    _quantize_cols(wd_ref[...]))

    @pl.when(live & (rows_valid <= sparse_rows))
    def _():
        wg, wu, wd = _quantize_cols(wg_ref[...]), _quantize_cols(wu_ref[...]), _quantize_cols(wd_ref[...])
        for r0 in range(0, sparse_rows, EXPERT_ROW_BLOCK):
            @pl.when(r0 < rows_valid)
            def _(r0=r0):
                swiglu_rows(slice(r0, r0 + EXPERT_ROW_BLOCK), wg, wu, wd)


def expert_ffn(hp, src, tile_expert, tile_rows, n_tiles, wg, wu, wd, tm=1024, tf=256):
    half = hp.shape[1]
    r = src.shape[0]
    d = 2 * half
    dff = wg.shape[2]
    nf = dff // tf
    fetch_steps = next(n for n in range(min(nf, tm // DMA_UNROLL), 0, -1) if (tm // DMA_UNROLL) % n == 0)
    rows_per_step = tm // fetch_steps

    def live_tile(i, nt):
        return jnp.minimum(i, nt[0] - 1)

    def wf(i, f, nt):
        return jnp.where(i < nt[0], f, nf - 1)

    kern = functools.partial(_expert_kernel, tm=tm, rows_per_step=rows_per_step)
    return pl.pallas_call(
        kern,
        grid_spec=pltpu.PrefetchScalarGridSpec(
            num_scalar_prefetch=4,
            grid=(r // tm, nf),
            in_specs=[
                pl.BlockSpec(memory_space=pl.ANY),
                pl.BlockSpec((None, d, tf), lambda i, f, te, tr, nt, src: (te[live_tile(i, nt)], 0, wf(i, f, nt))),
                pl.BlockSpec((None, d, tf), lambda i, f, te, tr, nt, src: (te[live_tile(i, nt)], 0, wf(i, f, nt))),
                pl.BlockSpec((None, tf, d), lambda i, f, te, tr, nt, src: (te[live_tile(i, nt)], wf(i, f, nt), 0)),
            ],
            out_specs=pl.BlockSpec((tm, d), lambda i, f, te, tr, nt, src: (i, 0)),
            scratch_shapes=[pltpu.VMEM((2, tm, half), jnp.uint32), pltpu.VMEM((tm, d), F8),
                            pltpu.SMEM((1,), F32), pltpu.SemaphoreType.DMA((2,))],
        ),
        out_shape=jax.ShapeDtypeStruct((r, d), F32),
        compiler_params=_cparams(("arbitrary", "arbitrary")),
        name="expert_ffn",
    )(tile_expert, tile_rows, n_tiles, src, hp, wg, wu, wd)


def _combine_kernel(pos_ref, y_ref, x_ref, w_ref, gt_ref, fg_ref, o_ref, buf, sem, *, tt):
    i = pl.program_id(0)
    slot = i % 2

    def fetch(tile, dst_slot):
        for k in range(TOP_K):
            _issue_rows(y_ref, pos_ref, tile * tt * TOP_K + k * tt, lambda r, k=k: buf.at[dst_slot, k, pl.ds(r, 1)],
                        sem.at[dst_slot], 0, tt)

    @pl.when(i == 0)
    def _():
        fetch(0, 0)

    @pl.when(i + 1 < pl.num_programs(0))
    def _():
        fetch(i + 1, 1 - slot)

    for k in range(TOP_K):
        pltpu.make_async_copy(y_ref.at[pl.ds(0, tt)], buf.at[slot, k], sem.at[slot]).wait()
    w = w_ref[...]
    y = buf[slot, 0] * w[:, 0:1]
    for k in range(1, TOP_K):
        y = y + buf[slot, k] * w[:, k:k + 1]
    x = x_ref[...] + gt_ref[...] * y
    ms = jnp.mean(x * x, axis=-1, keepdims=True)
    o_ref[...] = x * lax.rsqrt(ms + EPS) * fg_ref[...]


def combine_final(y_sorted, pos_kmajor, top_w, x2d, mod, seq, final_g, tt=256):
    r, d = x2d.shape
    tiles_per_seq = seq // tt
    kern = functools.partial(_combine_kernel, tt=tt)
    return pl.pallas_call(
        kern,
        grid_spec=pltpu.PrefetchScalarGridSpec(
            num_scalar_prefetch=1,
            grid=(r // tt,),
            in_specs=[
                pl.BlockSpec(memory_space=pl.ANY),
                pl.BlockSpec((tt, d), lambda i, pos: (i, 0)),
                pl.BlockSpec((tt, TOP_K), lambda i, pos: (i, 0)),
                pl.BlockSpec((None, 1, d), lambda i, pos: (i // tiles_per_seq, 0, 5)),
                pl.BlockSpec((1, d), lambda i, pos: (0, 0)),
            ],
            out_specs=pl.BlockSpec((tt, d), lambda i, pos: (i, 0)),
            scratch_shapes=[pltpu.VMEM((2, TOP_K, tt, d), F32), pltpu.SemaphoreType.DMA((2,))],
        ),
        out_shape=jax.ShapeDtypeStruct((r, d), F32),
        compiler_params=_cparams(("arbitrary",)),
        name="combine_final",
    )(pos_kmajor, y_sorted, x2d, top_w, mod, final_g.reshape(1, d))


def dispatch_plan(top_i, n_experts, tm, tt):
    t = top_i.shape[0]
    flat_e = top_i.reshape(-1)
    onehot = (flat_e[:, None] == jnp.arange(n_experts)[None, :]).astype(jnp.int32)
    rank = jnp.cumsum(onehot, axis=0) - onehot
    counts = jnp.sum(onehot, axis=0)
    tiles = (counts + tm - 1) // tm
    tile_end = jnp.cumsum(tiles)
    group_start = (tile_end - tiles) * tm
    pos = jnp.sum(onehot * (group_start[None, :] + rank), axis=1)
    n_rows = (t * TOP_K // tm + n_experts) * tm
    src = jnp.zeros((n_rows,), jnp.int32).at[pos].set(jnp.arange(t * TOP_K, dtype=jnp.int32) // TOP_K)
    tile_ids = jnp.arange(n_rows // tm)
    tile_expert = jnp.minimum(jnp.sum(tile_ids[:, None] >= tile_end[None, :], axis=1), n_experts - 1)
    tile_in_group = tile_ids - (tile_end - tiles)[tile_expert]
    tile_rows = jnp.where(tile_ids < tile_end[-1], jnp.clip(counts[tile_expert] - tile_in_group * tm, 0, tm), 0)
    pos_kmajor = pos.astype(jnp.int32).reshape(t // tt, tt, TOP_K).transpose(0, 2, 1).reshape(-1)
    return src, pos_kmajor, tile_expert.astype(jnp.int32), tile_rows.astype(jnp.int32), \
        tile_end[-1:].astype(jnp.int32)


def rope_head_order(w_cols):
    d, n = w_cols.shape
    quarter = HEAD_DIM // 4
    q = w_cols.reshape(d, n // HEAD_DIM, 4, quarter)
    return jnp.stack([q[:, :, 0], q[:, :, 2], q[:, :, 1], q[:, :, 3]], axis=2).reshape(d, n)


def rope_tables(seq):
    t = np.arange(seq)
    row, col = t // GRID_W, t % GRID_W
    quarter = HEAD_DIM // 4
    inv = ROPE_BASE ** (-np.arange(0, 2 * quarter, 2, dtype=np.float32) / (2 * quarter))
    lane = np.arange(HEAD_DIM)
    by_row = (lane // quarter) % 2 == 0
    pos = np.where(by_row[None, :], row[:, None], col[:, None]).astype(np.float32)
    ang = jnp.asarray(pos) * jnp.asarray(inv[lane % quarter])[None, :]
    cos, sin = jnp.cos(ang), jnp.sin(ang)
    return cos, jnp.where(jnp.asarray(lane < HEAD_DIM // 2)[None, :], -sin, sin)


def kernel(x, c, ctx, c_ctx, ada_w, ada_b, norm_g, final_g, ab_w_in, ab_rpb, ab_sinks, ab_w_out,
           ffn_w_gate, ffn_w_up, ffn_w_down, conv_w_pw1, conv_b_pw1, conv_w_dw, conv_b_dw,
           conv_ln_g, conv_ln_b, conv_w_pw2, conv_b_pw2, moe_w_router, moe_w_gate, moe_w_up, moe_w_down):
    b, s, d = x.shape
    n_ctx = ctx.shape[1]
    depth = ada_w.shape[0]
    assert depth == 2, "layer schedule below is written for one even and one odd layer"
    na_w = ab_rpb.shape[1] * HEAD_DIM
    sw_q = ab_sinks.shape[1] * HEAD_DIM
    in_w = ab_w_in.shape[2]
    sw_kv = (in_w - 3 * na_w - sw_q) // 2
    n_experts = moe_w_router.shape[2]

    mod_rows = 8
    cvec = jnp.zeros((mod_rows, d), F32).at[:b].set(c).at[b].set(c_ctx)
    mods = ada_mod_all(cvec, ada_w, ada_b).reshape(depth, mod_rows, 1, N_MOD * d)

    x2d = x.reshape(b * s, d)
    ctx2d = ctx.reshape(b * n_ctx, d)

    mod0 = mods[0]
    tabs = rope_tables(s)
    rope_lo, rope_hi = 3 * na_w, 3 * na_w + sw_q + sw_kv
    w_in = jnp.concatenate([ab_w_in[0][:, :rope_lo], rope_head_order(ab_w_in[0][:, rope_lo:rope_hi]),
                            ab_w_in[0][:, rope_hi:]], axis=1).astype(BF16)
    tm_in = min(1024, s)
    qkv = in_proj(x2d, norm_g[0, 0], mod0, lambda i: i // (s // tm_in), w_in, tabs, rope_lo, rope_hi, s, tm_in)
    ctx_tabs = tuple(t[:n_ctx] for t in tabs)
    qkv_ctx = in_proj(ctx2d, norm_g[0, 0], mod0, lambda i: b, w_in, ctx_tabs, 0, 0, n_ctx, n_ctx)
    qkv = qkv.reshape(b, s, in_w)
    qkv_ctx = qkv_ctx.reshape(b, n_ctx, in_w)

    na_heads = na_w // HEAD_DIM
    bias = na_bias_table(ab_rpb[0], s // GRID_W)
    y_a = na_attention(qkv, qkv_ctx, bias, na_heads, 0, na_heads, 2 * na_heads)
    y_b = band_attention(qkv, qkv_ctx, ab_sinks[0], sw_q // HEAD_DIM, sw_kv // HEAD_DIM,
                         3 * na_w, 3 * na_w + sw_q, 3 * na_w + sw_q + sw_kv)
    zero_d = jnp.zeros((d,), F32)
    x2d = proj_residual([y_a.reshape(b * s, na_w), y_b.reshape(b * s, sw_q)], ab_w_out[0].astype(BF16),
                        zero_d, x2d, mod0, 2, s)
    x2d = dense_ffn(x2d, norm_g[0, 1], mod0, s, ffn_w_gate[0].astype(BF16), ffn_w_up[0].astype(BF16),
                    ffn_w_down[0].astype(BF16))

    mod1 = mods[1]
    u = pw1_glu(x2d, norm_g[1, 0], mod1, s, conv_w_pw1[0].astype(BF16), conv_b_pw1[0])
    v = conv_ln_swish(u.reshape(b, s, d), conv_w_dw[0], conv_b_dw[0], conv_ln_g[0], conv_ln_b[0])
    x2d = proj_residual([v.reshape(b * s, d)], conv_w_pw2[0].astype(BF16), conv_b_pw2[0], x2d, mod1, 2, s)

    hp, top_i, top_w = router(x2d, norm_g[1, 1], mod1, s, moe_w_router[0])
    tm_e, tt = 1024, 256
    src, pos_kmajor, tile_expert, tile_rows, n_tiles = dispatch_plan(top_i, n_experts, tm_e, tt)
    y_sorted = expert_ffn(hp, src, tile_expert, tile_rows, n_tiles, moe_w_gate[0], moe_w_up[0], moe_w_down[0],
                          tm=tm_e)
    out = combine_final(y_sorted, pos_kmajor, top_w, x2d, mod1, s, final_g, tt=tt)
    return out.reshape(b, s, d)
```

```python
import functools

import numpy as np
import jax
import jax.numpy as jnp
from jax import lax
from jax.experimental import pallas as pl
from jax.experimental.pallas import tpu as pltpu

F32 = jnp.float32
BF16 = jnp.bfloat16
F8 = jnp.float8_e4m3fn
F8_MAX = 448.0
F8_TINY = 1e-30

GRID_W = 64
HEAD_DIM = 128
NA_KH = 8
NA_KW = 16
SW_WINDOW = 128
ROPE_BASE = 10000.0
CONV_WIDTH = 31
N_MOD = 6
TOP_K = 2
EPS = 1e-6
NEG = -1e30

VMEM_LIMIT = 56 * 1024 * 1024
NA_QROWS = 8
CONV_HALO = 16
CONV_COPY_ROWS, CONV_COPY_COLS = 64, 512
FFN_OUT_CHUNK = 512
SUBLANES = 8
EXPERT_ROW_BLOCK = 256
EXPERT_SPARSE_BLOCKS = 2
DMA_UNROLL = 16
ATTN_LOOKAHEAD = 3
NA_LOOKAHEAD = 5


def _pick_tile(n, candidates):
    return next(t for t in candidates if n % t == 0)


def _cparams(sem):
    return pltpu.CompilerParams(dimension_semantics=sem, vmem_limit_bytes=VMEM_LIMIT)


def _silu(v):
    return v / (1.0 + jnp.exp(-v))


def _sigmoid(v):
    return 1.0 / (1.0 + jnp.exp(-v))


def _norm_mod(x, g, sh, sc):
    ms = jnp.mean(x * x, axis=-1, keepdims=True)
    y = x * lax.rsqrt(ms + EPS) * g
    return y * (1.0 + sc) + sh


def _ada_kernel(c_ref, w_ref, b_ref, o_ref):
    s = _silu(c_ref[...]).astype(BF16)
    acc = jnp.dot(s, w_ref[...].astype(BF16), preferred_element_type=F32)
    o_ref[...] = acc + b_ref[...]


def ada_mod_all(cvec, ada_w, ada_b, tn=1024):
    depth, d, n = ada_w.shape
    rows = cvec.shape[0]
    return pl.pallas_call(
        _ada_kernel,
        grid=(depth, n // tn),
        in_specs=[
            pl.BlockSpec((rows, d), lambda l, j: (0, 0)),
            pl.BlockSpec((None, d, tn), lambda l, j: (l, 0, j)),
            pl.BlockSpec((None, 1, tn), lambda l, j: (l, 0, j)),
        ],
        out_specs=pl.BlockSpec((None, rows, tn), lambda l, j: (l, 0, j)),
        out_shape=jax.ShapeDtypeStruct((depth, rows, n), F32),
        compiler_params=_cparams(("parallel", "parallel")),
        name="ada_mod",
    )(cvec, ada_w, ada_b.reshape(depth, 1, n))


def _rope_head(v, cos, sn):
    return v * cos + pltpu.roll(v, HEAD_DIM // 2, 1) * sn


def _inproj_kernel(x_ref, g_ref, sh_ref, sc_ref, w_ref, cos_ref, sn_ref, o_ref, h_scr, *, rope_lo, rope_hi, tn):
    j = pl.program_id(1)

    @pl.when(j == 0)
    def _():
        h_scr[...] = _norm_mod(x_ref[...], g_ref[...], sh_ref[...], sc_ref[...]).astype(BF16)

    acc = jnp.dot(h_scr[...], w_ref[...], preferred_element_type=F32)
    heads = tn // HEAD_DIM
    first_tile = rope_lo // tn
    last_tile = (rope_hi - 1) // tn if rope_hi > rope_lo else -1

    def plain():
        o_ref[...] = acc.astype(o_ref.dtype)

    if rope_hi <= rope_lo:
        plain()
        return

    pl.when((j < first_tile) | (j > last_tile))(plain)

    for t in range(first_tile, last_tile + 1):
        def roped(t=t):
            cos, sn = cos_ref[...], sn_ref[...]
            for hh in range(heads):
                col = t * tn + hh * HEAD_DIM
                v = acc[:, hh * HEAD_DIM:(hh + 1) * HEAD_DIM]
                if rope_lo <= col < rope_hi:
                    v = _rope_head(v, cos, sn)
                o_ref[:, hh * HEAD_DIM:(hh + 1) * HEAD_DIM] = v.astype(o_ref.dtype)
        pl.when(j == t)(roped)


def in_proj(x2d, g, mod, mod_row_of_tile, w_bf16, rope_tabs, rope_lo, rope_hi, seq, tm):
    r, d = x2d.shape
    n = w_bf16.shape[1]
    tn = _pick_tile(n, (512, 256, 128))
    cos, sn = rope_tabs
    tiles_per_seq = seq // tm
    tab_spec = pl.BlockSpec((tm, HEAD_DIM), lambda i, j: (i % tiles_per_seq, 0))
    kern = functools.partial(_inproj_kernel, rope_lo=rope_lo, rope_hi=rope_hi, tn=tn)
    return pl.pallas_call(
        kern,
        grid=(r // tm, n // tn),
        in_specs=[
            pl.BlockSpec((tm, d), lambda i, j: (i, 0)),
            pl.BlockSpec((1, d), lambda i, j: (0, 0)),
            pl.BlockSpec((None, 1, d), lambda i, j: (mod_row_of_tile(i), 0, 0)),
            pl.BlockSpec((None, 1, d), lambda i, j: (mod_row_of_tile(i), 0, 1)),
            pl.BlockSpec((d, tn), lambda i, j: (0, j)),
            tab_spec, tab_spec,
        ],
        out_specs=pl.BlockSpec((tm, tn), lambda i, j: (i, j)),
        out_shape=jax.ShapeDtypeStruct((r, n), BF16),
        scratch_shapes=[pltpu.VMEM((tm, d), BF16)],
        compiler_params=_cparams(("parallel", "arbitrary")),
        name="in_proj",
    )(x2d, g.reshape(1, d), mod, mod, w_bf16, cos, sn)


def _glu_kernel(x_ref, g_ref, sh_ref, sc_ref, wa_ref, wg_ref, ba_ref, bg_ref, o_ref, h_scr):
    j = pl.program_id(1)

    @pl.when(j == 0)
    def _():
        h_scr[...] = _norm_mod(x_ref[...], g_ref[...], sh_ref[...], sc_ref[...]).astype(BF16)

    h = h_scr[...]
    a = jnp.dot(h, wa_ref[...], preferred_element_type=F32) + ba_ref[...]
    gate = jnp.dot(h, wg_ref[...], preferred_element_type=F32) + bg_ref[...]
    o_ref[...] = a * _sigmoid(gate)


def pw1_glu(x2d, g, mod, seq, w_bf16, b, tm=1024, tn=512):
    r, d = x2d.shape
    half = w_bf16.shape[1] // 2
    nj = half // tn
    tiles_per_seq = seq // tm
    b2 = b.reshape(1, 2 * half)
    return pl.pallas_call(
        _glu_kernel,
        grid=(r // tm, nj),
        in_specs=[
            pl.BlockSpec((tm, d), lambda i, j: (i, 0)),
            pl.BlockSpec((1, d), lambda i, j: (0, 0)),
            pl.BlockSpec((None, 1, d), lambda i, j: (i // tiles_per_seq, 0, 0)),
            pl.BlockSpec((None, 1, d), lambda i, j: (i // tiles_per_seq, 0, 1)),
            pl.BlockSpec((d, tn), lambda i, j: (0, j)),
            pl.BlockSpec((d, tn), lambda i, j: (0, j + nj)),
            pl.BlockSpec((1, tn), lambda i, j: (0, j)),
            pl.BlockSpec((1, tn), lambda i, j: (0, j + nj)),
        ],
        out_specs=pl.BlockSpec((tm, tn), lambda i, j: (i, j)),
        out_shape=jax.ShapeDtypeStruct((r, half), F32),
        scratch_shapes=[pltpu.VMEM((tm, d), BF16)],
        compiler_params=_cparams(("parallel", "arbitrary")),
        name="pw1_glu",
    )(x2d, g.reshape(1, d), mod, mod, w_bf16, w_bf16, b2, b2)


def _proj_res_kernel(*refs, n_in, splits):
    y_refs = refs[:n_in]
    w_ref, b_ref, x_ref, gt_ref, o_ref = refs[n_in:]
    acc = None
    for y_ref, (lo, hi) in zip(y_refs, splits):
        part = jnp.dot(y_ref[...], w_ref[lo:hi, :], preferred_element_type=F32)
        acc = part if acc is None else acc + part
    o_ref[...] = x_ref[...] + gt_ref[...] * (acc + b_ref[...])


def proj_residual(ys, w_bf16, b, x2d, mod, gate_chunk, seq, tm=512):
    r, d = x2d.shape
    kdim, n = w_bf16.shape
    tn = n
    tiles_per_seq = seq // tm
    splits, lo = [], 0
    for y in ys:
        splits.append((lo, lo + y.shape[1]))
        lo += y.shape[1]
    assert lo == kdim
    chunks_per_d = d // tn
    kern = functools.partial(_proj_res_kernel, n_in=len(ys), splits=tuple(splits))
    return pl.pallas_call(
        kern,
        grid=(r // tm, n // tn),
        in_specs=[pl.BlockSpec((tm, y.shape[1]), lambda i, j: (i, 0)) for y in ys] + [
            pl.BlockSpec((kdim, tn), lambda i, j: (0, j)),
            pl.BlockSpec((1, tn), lambda i, j: (0, j)),
            pl.BlockSpec((tm, tn), lambda i, j: (i, j)),
            pl.BlockSpec((None, 1, tn),
                         lambda i, j: (i // tiles_per_seq, 0, gate_chunk * chunks_per_d + j)),
        ],
        out_specs=pl.BlockSpec((tm, tn), lambda i, j: (i, j)),
        out_shape=jax.ShapeDtypeStruct((r, n), F32),
        compiler_params=_cparams(("parallel", "parallel")),
        name="proj_residual",
    )(*ys, w_bf16, b.reshape(1, n), x2d, mod)


def _na_kernel(q_ref, k_ref, v_ref, kc_ref, vc_ref, bias_ref, o_ref, *, scale, rows, kh):
    i = pl.program_id(2)
    nt = (((1,), (1,)), ((), ()))
    kc, vc = kc_ref[...], vc_ref[...]
    s_ctx = lax.dot_general(q_ref[...], kc, nt, preferred_element_type=F32) * scale
    def scores(rq):
        r = i * NA_QROWS + rq
        rs = jnp.clip(r - kh // 2, 0, rows - kh)
        start = pl.multiple_of(rs * GRID_W, GRID_W)
        qs = slice(rq * GRID_W, (rq + 1) * GRID_W)
        kw = k_ref[pl.ds(start, kh * GRID_W), :]
        s = lax.dot_general(q_ref[qs, :], kw, nt, preferred_element_type=F32) * scale + bias_ref[r - rs]
        return s, start

    def finish(rq, s, start):
        qs = slice(rq * GRID_W, (rq + 1) * GRID_W)
        vw = v_ref[pl.ds(start, kh * GRID_W), :]
        sc = s_ctx[qs, :]
        m = jnp.maximum(jnp.max(s, axis=-1, keepdims=True), jnp.max(sc, axis=-1, keepdims=True))
        p = jnp.exp(s - m)
        pc = jnp.exp(sc - m)
        l = jnp.sum(p, axis=-1, keepdims=True) + jnp.sum(pc, axis=-1, keepdims=True)
        o = (jnp.dot(p.astype(BF16), vw, preferred_element_type=F32)
             + jnp.dot(pc.astype(BF16), vc, preferred_element_type=F32))
        o_ref[qs, :] = (o / l).astype(o_ref.dtype)

    pending = [scores(rq) for rq in range(NA_LOOKAHEAD)]
    for rq in range(NA_QROWS):
        if rq + NA_LOOKAHEAD < NA_QROWS:
            pending.append(scores(rq + NA_LOOKAHEAD))
        finish(rq, *pending.pop(0))


def na_bias_table(rpb, rows):
    heads = rpb.shape[0]
    kh = min(NA_KH, rows)
    c = np.arange(GRID_W)
    dc = c[None, :] - c[:, None]
    col_start = np.clip(c - NA_KW // 2, 0, GRID_W - NA_KW)
    col_valid = (c[None, :] >= col_start[:, None]) & (c[None, :] < col_start[:, None] + NA_KW)
    col_idx = np.clip(dc + NA_KW - 1, 0, 2 * NA_KW - 2)
    toep = jnp.take(rpb, jnp.asarray(col_idx.reshape(-1)), axis=2).reshape(heads, 2 * NA_KH - 1, GRID_W, GRID_W)
    toep = jnp.where(jnp.asarray(col_valid)[None, None], toep, NEG).transpose(0, 2, 1, 3)
    tables = []
    for delta in range(kh):
        lo = NA_KH - 1 - delta
        tables.append(toep[:, :, lo:lo + kh, :].reshape(heads, GRID_W, kh * GRID_W))
    return jnp.stack(tables, axis=1)


def na_attention(qkv, qkv_ctx, bias, heads, q_blk0, k_blk0, v_blk0):
    b, s, _ = qkv.shape
    n_ctx = qkv_ctx.shape[1]
    rows = s // GRID_W
    kh = min(NA_KH, rows)
    tq = NA_QROWS * GRID_W
    kern = functools.partial(_na_kernel, scale=HEAD_DIM ** -0.5, rows=rows, kh=kh)
    return pl.pallas_call(
        kern,
        grid=(b, heads, rows // NA_QROWS),
        in_specs=[
            pl.BlockSpec((None, tq, HEAD_DIM), lambda bi, h, i: (bi, i, q_blk0 + h)),
            pl.BlockSpec((None, s, HEAD_DIM), lambda bi, h, i: (bi, 0, k_blk0 + h)),
            pl.BlockSpec((None, s, HEAD_DIM), lambda bi, h, i: (bi, 0, v_blk0 + h)),
            pl.BlockSpec((None, n_ctx, HEAD_DIM), lambda bi, h, i: (bi, 0, k_blk0 + h)),
            pl.BlockSpec((None, n_ctx, HEAD_DIM), lambda bi, h, i: (bi, 0, v_blk0 + h)),
            pl.BlockSpec((None, kh, GRID_W, kh * GRID_W), lambda bi, h, i: (h, 0, 0, 0)),
        ],
        out_specs=pl.BlockSpec((None, tq, HEAD_DIM), lambda bi, h, i: (bi, i, h)),
        out_shape=jax.ShapeDtypeStruct((b, s, heads * HEAD_DIM), BF16),
        compiler_params=_cparams(("parallel", "parallel", "arbitrary")),
        name="na_attention",
    )(qkv, qkv, qkv, qkv_ctx, qkv_ctx, bias)


def _band_kernel(q_ref, k_ref, v_ref, kc_ref, vc_ref, sink_ref, o_ref, kcat, vcat, *, scale, blk, group, seq):
    kvh = pl.program_id(1)
    n = pl.program_id(2)
    n_win = 3 * blk
    start = pl.multiple_of(jnp.clip((n - 1) * blk, 0, seq - n_win), blk)
    kcat[0:n_win, :] = k_ref[pl.ds(start, n_win), :]
    vcat[0:n_win, :] = v_ref[pl.ds(start, n_win), :]
    kcat[n_win:, :] = kc_ref[...]
    vcat[n_win:, :] = vc_ref[...]
    nk = kcat.shape[0]
    qpos = n * blk + lax.broadcasted_iota(jnp.int32, (blk, nk), 0)
    col = lax.broadcasted_iota(jnp.int32, (blk, nk), 1)
    kpos = start + col
    valid = (col >= n_win) | (jnp.abs(qpos - kpos) <= SW_WINDOW)
    kk, vv = kcat[...], vcat[...]
    def scores(g):
        hs = slice(g * HEAD_DIM, (g + 1) * HEAD_DIM)
        return lax.dot_general(q_ref[:, hs], kk, (((1,), (1,)), ((), ())), preferred_element_type=F32) * scale

    def finish(g, s):
        s = jnp.where(valid, s, NEG)
        sink = sink_ref[kvh * group + g]
        m = jnp.maximum(jnp.max(s, axis=-1, keepdims=True), sink)
        p = jnp.exp(s - m)
        l = jnp.sum(p, axis=-1, keepdims=True) + jnp.exp(sink - m)
        o = jnp.dot(p.astype(BF16), vv, preferred_element_type=F32) / l
        o_ref[:, g * HEAD_DIM:(g + 1) * HEAD_DIM] = o.astype(o_ref.dtype)

    pending = [scores(g) for g in range(ATTN_LOOKAHEAD)]
    for g in range(group):
        if g + ATTN_LOOKAHEAD < group:
            pending.append(scores(g + ATTN_LOOKAHEAD))
        finish(g, pending.pop(0))


def band_attention(qkv, qkv_ctx, sinks, q_heads, kv_heads, q_col0, k_col0, v_col0, blk=128):
    b, s, _ = qkv.shape
    n_ctx = qkv_ctx.shape[1]
    group = q_heads // kv_heads
    gw = group * HEAD_DIM
    nk = 3 * blk + n_ctx
    kern = functools.partial(_band_kernel, scale=HEAD_DIM ** -0.5, blk=blk, group=group, seq=s)
    return pl.pallas_call(
        kern,
        grid=(b, kv_heads, s // blk),
        in_specs=[
            pl.BlockSpec((None, blk, gw), lambda bi, kv, n: (bi, n, q_col0 // gw + kv)),
            pl.BlockSpec((None, s, HEAD_DIM), lambda bi, kv, n: (bi, 0, k_col0 // HEAD_DIM + kv)),
            pl.BlockSpec((None, s, HEAD_DIM), lambda bi, kv, n: (bi, 0, v_col0 // HEAD_DIM + kv)),
            pl.BlockSpec((None, n_ctx, HEAD_DIM), lambda bi, kv, n: (bi, 0, k_col0 // HEAD_DIM + kv)),
            pl.BlockSpec((None, n_ctx, HEAD_DIM), lambda bi, kv, n: (bi, 0, v_col0 // HEAD_DIM + kv)),
            pl.BlockSpec(memory_space=pltpu.SMEM),
        ],
        out_specs=pl.BlockSpec((None, blk, gw), lambda bi, kv, n: (bi, n, kv)),
        out_shape=jax.ShapeDtypeStruct((b, s, q_heads * HEAD_DIM), BF16),
        scratch_shapes=[pltpu.VMEM((nk, HEAD_DIM), BF16), pltpu.VMEM((nk, HEAD_DIM), BF16)],
        compiler_params=_cparams(("parallel", "parallel", "arbitrary")),
        name="band_attention",
    )(qkv, qkv, qkv, qkv_ctx, qkv_ctx, sinks)


def _quantize(v):
    amax = jnp.maximum(jnp.max(jnp.abs(v)), F8_TINY)
    return (v * (F8_MAX / amax)).astype(F8), amax * (1.0 / F8_MAX)


def _quantize_cols(w):
    amax = jnp.maximum(jnp.max(jnp.abs(w), axis=0, keepdims=True), F8_TINY)
    return (w * (F8_MAX / amax)).astype(F8), amax * (1.0 / F8_MAX)


def _ffn_kernel(x_ref, g_ref, sh_ref, sc_ref, gt_ref, wg_ref, wu_ref, wd_ref, o_ref, h_scr):
    f = pl.program_id(1)

    @pl.when(f == 0)
    def _():
        h_scr[...] = _norm_mod(x_ref[...], g_ref[...], sh_ref[...], sc_ref[...]).astype(BF16)
        o_ref[...] = jnp.zeros_like(o_ref)

    h = h_scr[...]
    gate = jnp.dot(h, wg_ref[...], preferred_element_type=F32)
    up = jnp.dot(h, wu_ref[...], preferred_element_type=F32)
    a = (_silu(gate) * up).astype(BF16)
    for c0 in range(0, o_ref.shape[1], FFN_OUT_CHUNK):
        cs = slice(c0, c0 + FFN_OUT_CHUNK)
        o_ref[:, cs] += jnp.dot(a, wd_ref[:, cs], preferred_element_type=F32)

    @pl.when(f == pl.num_programs(1) - 1)
    def _():
        o_ref[...] = x_ref[...] + gt_ref[...] * o_ref[...]


def dense_ffn(x2d, g, mod, seq, wg, wu, wd, tm=1024, tf=512):
    r, d = x2d.shape
    dff = wg.shape[1]
    tiles_per_seq = seq // tm
    row = lambda i, f: i // tiles_per_seq
    return pl.pallas_call(
        _ffn_kernel,
        grid=(r // tm, dff // tf),
        in_specs=[
            pl.BlockSpec((tm, d), lambda i, f: (i, 0), pipeline_mode=pl.Buffered(1)),
            pl.BlockSpec((1, d), lambda i, f: (0, 0)),
            pl.BlockSpec((None, 1, d), lambda i, f: (row(i, f), 0, 3)),
            pl.BlockSpec((None, 1, d), lambda i, f: (row(i, f), 0, 4)),
            pl.BlockSpec((None, 1, d), lambda i, f: (row(i, f), 0, 5)),
            pl.BlockSpec((d, tf), lambda i, f: (0, f)),
            pl.BlockSpec((d, tf), lambda i, f: (0, f)),
            pl.BlockSpec((tf, d), lambda i, f: (f, 0)),
        ],
        out_specs=pl.BlockSpec((tm, d), lambda i, f: (i, 0)),
        out_shape=jax.ShapeDtypeStruct((r, d), F32),
        scratch_shapes=[pltpu.VMEM((tm, d), BF16)],
        compiler_params=_cparams(("parallel", "arbitrary")),
        name="dense_ffn",
    )(x2d, g.reshape(1, d), mod, mod, mod, wg, wu, wd)


def _conv_kernel(prev_ref, cur_ref, next_ref, w_ref, b_ref, g_ref, beta_ref, o_ref, ubuf, cbuf, *, ts, rb, cb):
    i = pl.program_id(1)
    last = pl.num_programs(1) - 1
    d = cur_ref.shape[1]
    zeros = jnp.zeros((CONV_HALO, d), F32)

    @pl.when(i == 0)
    def _():
        ubuf[0, 0:CONV_HALO, :] = zeros

    @pl.when(i > 0)
    def _():
        ubuf[0, 0:CONV_HALO, :] = prev_ref[...]

    ubuf[0, CONV_HALO:CONV_HALO + ts, :] = cur_ref[...]

    @pl.when(i == last)
    def _():
        ubuf[0, CONV_HALO + ts:, :] = zeros

    @pl.when(i < last)
    def _():
        ubuf[0, CONV_HALO + ts:, :] = next_ref[...]

    off = CONV_HALO - CONV_WIDTH // 2
    rows_read = ts + (off + CONV_WIDTH - 1) // SUBLANES * SUBLANES
    for s in range(1, SUBLANES):
        for c0 in range(0, d, CONV_COPY_COLS):
            for a in range(0, rows_read, CONV_COPY_ROWS):
                n = min(CONV_COPY_ROWS, rows_read - a)
                ubuf[s, a:a + n, c0:c0 + CONV_COPY_COLS] = ubuf[0, a + s:a + s + n, c0:c0 + CONV_COPY_COLS]

    for c0 in range(0, d, cb):
        wt = w_ref[:, c0:c0 + cb]
        bias = b_ref[:, c0:c0 + cb]

        def row_body(rt, carry, c0=c0, wt=wt, bias=bias):
            r0 = pl.multiple_of(rt * rb, rb)
            acc = jnp.zeros((rb, cb), F32) + bias
            for k in range(CONV_WIDTH):
                s, q = (off + k) % SUBLANES, (off + k) // SUBLANES
                acc = acc + ubuf[s, pl.ds(r0 + q * SUBLANES, rb), c0:c0 + cb] * wt[k:k + 1, :]
            cbuf[pl.ds(r0, rb), c0:c0 + cb] = acc
            return carry

        lax.fori_loop(0, ts // rb, row_body, 0)

    v = cbuf[...]
    mu = jnp.mean(v, axis=-1, keepdims=True)
    cen = v - mu
    var = jnp.mean(cen * cen, axis=-1, keepdims=True)
    y = cen * lax.rsqrt(var + EPS) * g_ref[...] + beta_ref[...]
    o_ref[...] = _silu(y).astype(o_ref.dtype)


def conv_ln_swish(u, w_dw, b_dw, ln_g, ln_b, ts=256, rb=32, cb=256):
    b, s, d = u.shape
    hb = ts // CONV_HALO
    n_halo = s // CONV_HALO
    kern = functools.partial(_conv_kernel, ts=ts, rb=rb, cb=cb)
    return pl.pallas_call(
        kern,
        grid=(b, s // ts),
        in_specs=[
            pl.BlockSpec((None, CONV_HALO, d), lambda bi, i: (bi, jnp.maximum(i * hb - 1, 0), 0)),
            pl.BlockSpec((None, ts, d), lambda bi, i: (bi, i, 0)),
            pl.BlockSpec((None, CONV_HALO, d), lambda bi, i: (bi, jnp.minimum((i + 1) * hb, n_halo - 1), 0)),
            pl.BlockSpec((CONV_WIDTH, d), lambda bi, i: (0, 0)),
            pl.BlockSpec((1, d), lambda bi, i: (0, 0)),
            pl.BlockSpec((1, d), lambda bi, i: (0, 0)),
            pl.BlockSpec((1, d), lambda bi, i: (0, 0)),
        ],
        out_specs=pl.BlockSpec((None, ts, d), lambda bi, i: (bi, i, 0)),
        out_shape=jax.ShapeDtypeStruct((b, s, d), BF16),
        scratch_shapes=[pltpu.VMEM((SUBLANES, ts + 2 * CONV_HALO, d), F32), pltpu.VMEM((ts, d), F32)],
        compiler_params=_cparams(("parallel", "arbitrary")),
        name="conv_ln_swish",
    )(u, u, u, w_dw, b_dw.reshape(1, d), ln_g.reshape(1, d), ln_b.reshape(1, d))


def _bf16_bits(v):
    b = pltpu.bitcast(v, jnp.uint32)
    return (b + jnp.uint32(0x7FFF) + ((b >> 16) & jnp.uint32(1))) >> 16


def _router_kernel(x_ref, g_ref, sh_ref, sc_ref, wh_ref, wl_ref, hp_ref, idx_ref, wt_ref, *, n_experts):
    h = _norm_mod(x_ref[...], g_ref[...], sh_ref[...], sc_ref[...])
    half = h.shape[1] // 2
    hp_ref[...] = (_bf16_bits(h[:, half:]) << 16) | _bf16_bits(h[:, :half])
    h_hi = h.astype(BF16)
    h_lo = (h - h_hi.astype(F32)).astype(BF16)
    wh, wl = wh_ref[...], wl_ref[...]
    logits = (jnp.dot(h_hi, wh, preferred_element_type=F32)
              + (jnp.dot(h_hi, wl, preferred_element_type=F32) + jnp.dot(h_lo, wh, preferred_element_type=F32)))
    lane = lax.broadcasted_iota(jnp.int32, logits.shape, 1)
    big = logits.shape[1]
    logits = jnp.where(lane < n_experts, logits, -jnp.inf)
    m1 = jnp.max(logits, axis=-1, keepdims=True)
    i1 = jnp.min(jnp.where(logits == m1, lane, big), axis=-1, keepdims=True)
    rest = jnp.where(lane == i1, -jnp.inf, logits)
    m2 = jnp.max(rest, axis=-1, keepdims=True)
    i2 = jnp.min(jnp.where(rest == m2, lane, big), axis=-1, keepdims=True)
    e = jnp.exp(m2 - m1)
    w1 = 1.0 / (1.0 + e)
    w2 = e / (1.0 + e)
    idx_ref[...] = jnp.where(lane == 0, i1, jnp.where(lane == 1, i2, 0))
    wt_ref[...] = jnp.where(lane == 0, w1, jnp.where(lane == 1, w2, 0.0))


def router(x2d, g, mod, seq, w_router, tm=512):
    r, d = x2d.shape
    n_experts = w_router.shape[1]
    lanes = 128
    wpad = jnp.zeros((d, lanes), F32).at[:, :n_experts].set(w_router)
    wh = wpad.astype(BF16)
    wl = (wpad - wh.astype(F32)).astype(BF16)
    tiles_per_seq = seq // tm
    kern = functools.partial(_router_kernel, n_experts=n_experts)
    hp, idx, wt = pl.pallas_call(
        kern,
        grid=(r // tm,),
        in_specs=[
            pl.BlockSpec((tm, d), lambda i: (i, 0)),
            pl.BlockSpec((1, d), lambda i: (0, 0)),
            pl.BlockSpec((None, 1, d), lambda i: (i // tiles_per_seq, 0, 3)),
            pl.BlockSpec((None, 1, d), lambda i: (i // tiles_per_seq, 0, 4)),
            pl.BlockSpec((d, lanes), lambda i: (0, 0)),
            pl.BlockSpec((d, lanes), lambda i: (0, 0)),
        ],
        out_specs=[
            pl.BlockSpec((tm, d // 2), lambda i: (i, 0)),
            pl.BlockSpec((tm, lanes), lambda i: (i, 0)),
            pl.BlockSpec((tm, lanes), lambda i: (i, 0)),
        ],
        out_shape=[
            jax.ShapeDtypeStruct((r, d // 2), jnp.uint32),
            jax.ShapeDtypeStruct((r, lanes), jnp.int32),
            jax.ShapeDtypeStruct((r, lanes), F32),
        ],
        compiler_params=_cparams(("parallel",)),
        name="router",
    )(x2d, g.reshape(1, d), mod, mod, wh, wl)
    return hp, idx[:, :TOP_K], wt[:, :TOP_K]


def _issue_rows(table_ref, idx_ref, idx_base, dst_of_row, sem, row0, n_rows):
    def body(g, carry):
        base = pl.multiple_of(row0 + g * DMA_UNROLL, DMA_UNROLL)
        for u in range(DMA_UNROLL):
            r = base + u
            pltpu.make_async_copy(table_ref.at[pl.ds(idx_ref[idx_base + r], 1)], dst_of_row(r), sem).start()
        return carry

    lax.fori_loop(0, n_rows // DMA_UNROLL, body, 0)


def _expert_kernel(te_ref, tr_ref, nt_ref, src_ref, hp_ref, wg_ref, wu_ref, wd_ref, o_ref, xbuf, x_scr, xs_scr,
                   sem, *, tm, rows_per_step):
    i = pl.program_id(0)
    f = pl.program_id(1)
    n_live = nt_ref[0]
    live = i < n_live

    @pl.when(jnp.logical_not(live) & (f == 0))
    def _():
        o_ref[...] = jnp.zeros_like(o_ref)

    @pl.when((i == 0) & (f == 0))
    def _():
        _issue_rows(hp_ref, src_ref, 0, lambda r: xbuf.at[pl.ds(r, 1)], sem, 0, tm)

    @pl.when(live & (f == 0))
    def _():
        pltpu.make_async_copy(hp_ref.at[pl.ds(0, tm)], xbuf, sem).wait()
        xp = xbuf[...]
        lo = pltpu.bitcast(xp << 16, F32)
        hi = pltpu.bitcast(xp & jnp.uint32(0xFFFF0000), F32)
        half = xp.shape[1]
        x_inv = jnp.maximum(jnp.maximum(jnp.max(jnp.abs(lo)), jnp.max(jnp.abs(hi))), F8_TINY) * (1.0 / F8_MAX)
        x_scr[:, :half] = (lo * (1.0 / x_inv)).astype(F8)
        x_scr[:, half:] = (hi * (1.0 / x_inv)).astype(F8)
        xs_scr[0] = x_inv
        o_ref[...] = jnp.zeros_like(o_ref)

    @pl.when((i + 1 < n_live) & (f < tm // rows_per_step))
    def _():
        _issue_rows(hp_ref, src_ref, (i + 1) * tm, lambda r: xbuf.at[pl.ds(r, 1)], sem, f * rows_per_step,
                    rows_per_step)

    def swiglu_rows(rs, wg, wu, wd):
        (wg8, sg), (wu8, su), (wd8, sd) = wg, wu, wd
        x = x_scr[rs, :]
        x_inv = xs_scr[0]
        gate = jnp.dot(x, wg8, preferred_element_type=F32) * (sg * x_inv)
        up = jnp.dot(x, wu8, preferred_element_type=F32) * (su * x_inv)
        a, a_inv = _quantize(_silu(gate) * up)
        for c0 in range(0, o_ref.shape[1], FFN_OUT_CHUNK):
            cs = slice(c0, c0 + FFN_OUT_CHUNK)
            o_ref[rs, cs] += jnp.dot(a, wd8[:, cs], preferred_element_type=F32) * (sd[:, cs] * a_inv)

    rows_valid = tr_ref[i]
    sparse_rows = EXPERT_SPARSE_BLOCKS * EXPERT_ROW_BLOCK

    @pl.when(live & (rows_valid > sparse_rows))
    def _():
        swiglu_rows(slice(0, tm), _quantize_cols(wg_ref[...]), _quantize_cols(wu_ref[...]),
                    _quantize_cols(wd_ref[...]))

    @pl.when(live & (rows_valid <= sparse_rows))
    def _():
        wg, wu, wd = _quantize_cols(wg_ref[...]), _quantize_cols(wu_ref[...]), _quantize_cols(wd_ref[...])
        for r0 in range(0, sparse_rows, EXPERT_ROW_BLOCK):
            @pl.when(r0 < rows_valid)
            def _(r0=r0):
                swiglu_rows(slice(r0, r0 + EXPERT_ROW_BLOCK), wg, wu, wd)


def expert_ffn(hp, src, tile_expert, tile_rows, n_tiles, wg, wu, wd, tm=1024, tf=512):
    half = hp.shape[1]
    r = src.shape[0]
    d = 2 * half
    dff = wg.shape[2]
    nf = dff // tf
    fetch_steps = next(n for n in range(min(nf, tm // DMA_UNROLL), 0, -1) if (tm // DMA_UNROLL) % n == 0)
    rows_per_step = tm // fetch_steps

    def live_tile(i, nt):
        return jnp.minimum(i, nt[0] - 1)

    def wf(i, f, nt):
        return jnp.where(i < nt[0], f, nf - 1)

    kern = functools.partial(_expert_kernel, tm=tm, rows_per_step=rows_per_step)
    return pl.pallas_call(
        kern,
        grid_spec=pltpu.PrefetchScalarGridSpec(
            num_scalar_prefetch=4,
            grid=(r // tm, nf),
            in_specs=[
                pl.BlockSpec(memory_space=pl.ANY),
                pl.BlockSpec((None, d, tf), lambda i, f, te, tr, nt, src: (te[live_tile(i, nt)], 0, wf(i, f, nt))),
                pl.BlockSpec((None, d, tf), lambda i, f, te, tr, nt, src: (te[live_tile(i, nt)], 0, wf(i, f, nt))),
                pl.BlockSpec((None, tf, d), lambda i, f, te, tr, nt, src: (te[live_tile(i, nt)], wf(i, f, nt), 0)),
            ],
            out_specs=pl.BlockSpec((tm, d), lambda i, f, te, tr, nt, src: (i, 0), pipeline_mode=pl.Buffered(1)),
            scratch_shapes=[pltpu.VMEM((tm, half), jnp.uint32), pltpu.VMEM((tm, d), F8),
                            pltpu.SMEM((1,), F32), pltpu.SemaphoreType.DMA(())],
        ),
        out_shape=jax.ShapeDtypeStruct((r, d), F32),
        compiler_params=_cparams(("arbitrary", "arbitrary")),
        name="expert_ffn",
    )(tile_expert, tile_rows, n_tiles, src, hp, wg, wu, wd)


def _combine_kernel(pos_ref, y_ref, x_ref, w_ref, gt_ref, fg_ref, o_ref, buf, sem, *, tt):
    i = pl.program_id(0)
    slot = i % 2

    def fetch(tile, dst_slot):
        for k in range(TOP_K):
            _issue_rows(y_ref, pos_ref, tile * tt * TOP_K + k * tt, lambda r, k=k: buf.at[dst_slot, k, pl.ds(r, 1)],
                        sem.at[dst_slot], 0, tt)

    @pl.when(i == 0)
    def _():
        fetch(0, 0)

    @pl.when(i + 1 < pl.num_programs(0))
    def _():
        fetch(i + 1, 1 - slot)

    for k in range(TOP_K):
        pltpu.make_async_copy(y_ref.at[pl.ds(0, tt)], buf.at[slot, k], sem.at[slot]).wait()
    w = w_ref[...]
    y = buf[slot, 0] * w[:, 0:1]
    for k in range(1, TOP_K):
        y = y + buf[slot, k] * w[:, k:k + 1]
    x = x_ref[...] + gt_ref[...] * y
    ms = jnp.mean(x * x, axis=-1, keepdims=True)
    o_ref[...] = x * lax.rsqrt(ms + EPS) * fg_ref[...]


def combine_final(y_sorted, pos_kmajor, top_w, x2d, mod, seq, final_g, tt=256):
    r, d = x2d.shape
    tiles_per_seq = seq // tt
    kern = functools.partial(_combine_kernel, tt=tt)
    return pl.pallas_call(
        kern,
        grid_spec=pltpu.PrefetchScalarGridSpec(
            num_scalar_prefetch=1,
            grid=(r // tt,),
            in_specs=[
                pl.BlockSpec(memory_space=pl.ANY),
                pl.BlockSpec((tt, d), lambda i, pos: (i, 0)),
                pl.BlockSpec((tt, TOP_K), lambda i, pos: (i, 0)),
                pl.BlockSpec((None, 1, d), lambda i, pos: (i // tiles_per_seq, 0, 5)),
                pl.BlockSpec((1, d), lambda i, pos: (0, 0)),
            ],
            out_specs=pl.BlockSpec((tt, d), lambda i, pos: (i, 0)),
            scratch_shapes=[pltpu.VMEM((2, TOP_K, tt, d), F32), pltpu.SemaphoreType.DMA((2,))],
        ),
        out_shape=jax.ShapeDtypeStruct((r, d), F32),
        compiler_params=_cparams(("arbitrary",)),
        name="combine_final",
    )(pos_kmajor, y_sorted, x2d, top_w, mod, final_g.reshape(1, d))


def dispatch_plan(top_i, n_experts, tm, tt):
    t = top_i.shape[0]
    flat_e = top_i.reshape(-1)
    onehot = (flat_e[:, None] == jnp.arange(n_experts)[None, :]).astype(jnp.int32)
    rank = jnp.cumsum(onehot, axis=0) - onehot
    counts = jnp.sum(onehot, axis=0)
    tiles = (counts + tm - 1) // tm
    tile_end = jnp.cumsum(tiles)
    group_start = (tile_end - tiles) * tm
    pos = jnp.sum(onehot * (group_start[None, :] + rank), axis=1)
    n_rows = (t * TOP_K // tm + n_experts) * tm
    src = jnp.zeros((n_rows,), jnp.int32).at[pos].set(jnp.arange(t * TOP_K, dtype=jnp.int32) // TOP_K)
    tile_ids = jnp.arange(n_rows // tm)
    tile_expert = jnp.minimum(jnp.sum(tile_ids[:, None] >= tile_end[None, :], axis=1), n_experts - 1)
    tile_in_group = tile_ids - (tile_end - tiles)[tile_expert]
    tile_rows = jnp.where(tile_ids < tile_end[-1], jnp.clip(counts[tile_expert] - tile_in_group * tm, 0, tm), 0)
    pos_kmajor = pos.astype(jnp.int32).reshape(t // tt, tt, TOP_K).transpose(0, 2, 1).reshape(-1)
    return src, pos_kmajor, tile_expert.astype(jnp.int32), tile_rows.astype(jnp.int32), \
        tile_end[-1:].astype(jnp.int32)


def rope_head_order(w_cols):
    d, n = w_cols.shape
    quarter = HEAD_DIM // 4
    q = w_cols.reshape(d, n // HEAD_DIM, 4, quarter)
    return jnp.stack([q[:, :, 0], q[:, :, 2], q[:, :, 1], q[:, :, 3]], axis=2).reshape(d, n)


def rope_tables(seq):
    t = np.arange(seq)
    row, col = t // GRID_W, t % GRID_W
    quarter = HEAD_DIM // 4
    inv = ROPE_BASE ** (-np.arange(0, 2 * quarter, 2, dtype=np.float32) / (2 * quarter))
    lane = np.arange(HEAD_DIM)
    by_row = (lane // quarter) % 2 == 0
    pos = np.where(by_row[None, :], row[:, None], col[:, None]).astype(np.float32)
    ang = jnp.asarray(pos) * jnp.asarray(inv[lane % quarter])[None, :]
    cos, sin = jnp.cos(ang), jnp.sin(ang)
    return cos, jnp.where(jnp.asarray(lane < HEAD_DIM // 2)[None, :], -sin, sin)


def kernel(x, c, ctx, c_ctx, ada_w, ada_b, norm_g, final_g, ab_w_in, ab_rpb, ab_sinks, ab_w_out,
           ffn_w_gate, ffn_w_up, ffn_w_down, conv_w_pw1, conv_b_pw1, conv_w_dw, conv_b_dw,
           conv_ln_g, conv_ln_b, conv_w_pw2, conv_b_pw2, moe_w_router, moe_w_gate, moe_w_up, moe_w_down):
    b, s, d = x.shape
    n_ctx = ctx.shape[1]
    depth = ada_w.shape[0]
    assert depth == 2, "layer schedule below is written for one even and one odd layer"
    na_w = ab_rpb.shape[1] * HEAD_DIM
    sw_q = ab_sinks.shape[1] * HEAD_DIM
    in_w = ab_w_in.shape[2]
    sw_kv = (in_w - 3 * na_w - sw_q) // 2
    n_experts = moe_w_router.shape[2]

    mod_rows = 8
    cvec = jnp.zeros((mod_rows, d), F32).at[:b].set(c).at[b].set(c_ctx)
    mods = ada_mod_all(cvec, ada_w, ada_b).reshape(depth, mod_rows, 1, N_MOD * d)

    x2d = x.reshape(b * s, d)
    ctx2d = ctx.reshape(b * n_ctx, d)

    mod0 = mods[0]
    tabs = rope_tables(s)
    rope_lo, rope_hi = 3 * na_w, 3 * na_w + sw_q + sw_kv
    w_in = jnp.concatenate([ab_w_in[0][:, :rope_lo], rope_head_order(ab_w_in[0][:, rope_lo:rope_hi]),
                            ab_w_in[0][:, rope_hi:]], axis=1).astype(BF16)
    tm_in = min(1024, s)
    qkv = in_proj(x2d, norm_g[0, 0], mod0, lambda i: i // (s // tm_in), w_in, tabs, rope_lo, rope_hi, s, tm_in)
    ctx_tabs = tuple(t[:n_ctx] for t in tabs)
    qkv_ctx = in_proj(ctx2d, norm_g[0, 0], mod0, lambda i: b, w_in, ctx_tabs, 0, 0, n_ctx, n_ctx)
    qkv = qkv.reshape(b, s, in_w)
    qkv_ctx = qkv_ctx.reshape(b, n_ctx, in_w)

    na_heads = na_w // HEAD_DIM
    bias = na_bias_table(ab_rpb[0], s // GRID_W)
    y_a = na_attention(qkv, qkv_ctx, bias, na_heads, 0, na_heads, 2 * na_heads)
    y_b = band_attention(qkv, qkv_ctx, ab_sinks[0], sw_q // HEAD_DIM, sw_kv // HEAD_DIM,
                         3 * na_w, 3 * na_w + sw_q, 3 * na_w + sw_q + sw_kv)
    zero_d = jnp.zeros((d,), F32)
    x2d = proj_residual([y_a.reshape(b * s, na_w), y_b.reshape(b * s, sw_q)], ab_w_out[0].astype(BF16),
                        zero_d, x2d, mod0, 2, s)
    x2d = dense_ffn(x2d, norm_g[0, 1], mod0, s, ffn_w_gate[0].astype(BF16), ffn_w_up[0].astype(BF16),
                    ffn_w_down[0].astype(BF16))

    mod1 = mods[1]
    u = pw1_glu(x2d, norm_g[1, 0], mod1, s, conv_w_pw1[0].astype(BF16), conv_b_pw1[0])
    v = conv_ln_swish(u.reshape(b, s, d), conv_w_dw[0], conv_b_dw[0], conv_ln_g[0], conv_ln_b[0])
    x2d = proj_residual([v.reshape(b * s, d)], conv_w_pw2[0].astype(BF16), conv_b_pw2[0], x2d, mod1, 2, s)

    hp, top_i, top_w = router(x2d, norm_g[1, 1], mod1, s, moe_w_router[0])
    tm_e, tt = 1024, 256
    src, pos_kmajor, tile_expert, tile_rows, n_tiles = dispatch_plan(top_i, n_experts, tm_e, tt)
    y_sorted = expert_ffn(hp, src, tile_expert, tile_rows, n_tiles, moe_w_gate[0], moe_w_up[0], moe_w_down[0],
                          tm=tm_e)
    out = combine_final(y_sorted, pos_kmajor, top_w, x2d, mod1, s, final_g, tt=tt)
    return out.reshape(b, s, d)
```

```python
import functools

import numpy as np
import jax
import jax.numpy as jnp
from jax import lax
from jax.experimental import pallas as pl
from jax.experimental.pallas import tpu as pltpu

F32 = jnp.float32
BF16 = jnp.bfloat16
F8 = jnp.float8_e4m3fn
F8_MAX = 448.0
F8_TINY = 1e-30

GRID_W = 64
HEAD_DIM = 128
NA_KH = 8
NA_KW = 16
SW_WINDOW = 128
ROPE_BASE = 10000.0
CONV_WIDTH = 31
N_MOD = 6
TOP_K = 2
EPS = 1e-6
NEG = -1e30

VMEM_LIMIT = 56 * 1024 * 1024
NA_QROWS = 8
CONV_HALO = 16
CONV_COPY_ROWS, CONV_COPY_COLS = 64, 512
FFN_OUT_CHUNK = 512
SUBLANES = 8
EXPERT_ROW_BLOCK = 256
EXPERT_SPARSE_BLOCKS = 2
DMA_UNROLL = 16
ATTN_LOOKAHEAD = 3
NA_LOOKAHEAD = 5


def _pick_tile(n, candidates):
    return next(t for t in candidates if n % t == 0)


def _cparams(sem):
    return pltpu.CompilerParams(dimension_semantics=sem, vmem_limit_bytes=VMEM_LIMIT)


def _silu(v):
    return v / (1.0 + jnp.exp(-v))


def _sigmoid(v):
    return 1.0 / (1.0 + jnp.exp(-v))


def _norm_mod(x, g, sh, sc):
    ms = jnp.mean(x * x, axis=-1, keepdims=True)
    y = x * lax.rsqrt(ms + EPS) * g
    return y * (1.0 + sc) + sh


def _ada_kernel(c_ref, w_ref, b_ref, o_ref):
    s = _silu(c_ref[...]).astype(BF16)
    acc = jnp.dot(s, w_ref[...].astype(BF16), preferred_element_type=F32)
    o_ref[...] = acc + b_ref[...]


def ada_mod_all(cvec, ada_w, ada_b, tn=1024):
    depth, d, n = ada_w.shape
    rows = cvec.shape[0]
    return pl.pallas_call(
        _ada_kernel,
        grid=(depth, n // tn),
        in_specs=[
            pl.BlockSpec((rows, d), lambda l, j: (0, 0)),
            pl.BlockSpec((None, d, tn), lambda l, j: (l, 0, j)),
            pl.BlockSpec((None, 1, tn), lambda l, j: (l, 0, j)),
        ],
        out_specs=pl.BlockSpec((None, rows, tn), lambda l, j: (l, 0, j)),
        out_shape=jax.ShapeDtypeStruct((depth, rows, n), F32),
        compiler_params=_cparams(("parallel", "parallel")),
        name="ada_mod",
    )(cvec, ada_w, ada_b.reshape(depth, 1, n))


def _rope_head(v, cos, sn):
    return v * cos + pltpu.roll(v, HEAD_DIM // 2, 1) * sn


def _inproj_kernel(x_ref, g_ref, sh_ref, sc_ref, w_ref, cos_ref, sn_ref, o_ref, h_scr, *, rope_lo, rope_hi, tn):
    j = pl.program_id(1)

    @pl.when(j == 0)
    def _():
        h_scr[...] = _norm_mod(x_ref[...], g_ref[...], sh_ref[...], sc_ref[...]).astype(BF16)

    acc = jnp.dot(h_scr[...], w_ref[...], preferred_element_type=F32)
    heads = tn // HEAD_DIM
    first_tile = rope_lo // tn
    last_tile = (rope_hi - 1) // tn if rope_hi > rope_lo else -1

    def plain():
        o_ref[...] = acc.astype(o_ref.dtype)

    if rope_hi <= rope_lo:
        plain()
        return

    pl.when((j < first_tile) | (j > last_tile))(plain)

    for t in range(first_tile, last_tile + 1):
        def roped(t=t):
            cos, sn = cos_ref[...], sn_ref[...]
            for hh in range(heads):
                col = t * tn + hh * HEAD_DIM
                v = acc[:, hh * HEAD_DIM:(hh + 1) * HEAD_DIM]
                if rope_lo <= col < rope_hi:
                    v = _rope_head(v, cos, sn)
                o_ref[:, hh * HEAD_DIM:(hh + 1) * HEAD_DIM] = v.astype(o_ref.dtype)
        pl.when(j == t)(roped)


def in_proj(x2d, g, mod, mod_row_of_tile, w_bf16, rope_tabs, rope_lo, rope_hi, seq, tm):
    r, d = x2d.shape
    n = w_bf16.shape[1]
    tn = _pick_tile(n, (512, 256, 128))
    cos, sn = rope_tabs
    tiles_per_seq = seq // tm
    tab_spec = pl.BlockSpec((tm, HEAD_DIM), lambda i, j: (i % tiles_per_seq, 0))
    kern = functools.partial(_inproj_kernel, rope_lo=rope_lo, rope_hi=rope_hi, tn=tn)
    return pl.pallas_call(
        kern,
        grid=(r // tm, n // tn),
        in_specs=[
            pl.BlockSpec((tm, d), lambda i, j: (i, 0)),
            pl.BlockSpec((1, d), lambda i, j: (0, 0)),
            pl.BlockSpec((None, 1, d), lambda i, j: (mod_row_of_tile(i), 0, 0)),
            pl.BlockSpec((None, 1, d), lambda i, j: (mod_row_of_tile(i), 0, 1)),
            pl.BlockSpec((d, tn), lambda i, j: (0, j)),
            tab_spec, tab_spec,
        ],
        out_specs=pl.BlockSpec((tm, tn), lambda i, j: (i, j)),
        out_shape=jax.ShapeDtypeStruct((r, n), BF16),
        scratch_shapes=[pltpu.VMEM((tm, d), BF16)],
        compiler_params=_cparams(("parallel", "arbitrary")),
        name="in_proj",
    )(x2d, g.reshape(1, d), mod, mod, w_bf16, cos, sn)


def _glu_kernel(x_ref, g_ref, sh_ref, sc_ref, wa_ref, wg_ref, ba_ref, bg_ref, o_ref, h_scr):
    j = pl.program_id(1)

    @pl.when(j == 0)
    def _():
        h_scr[...] = _norm_mod(x_ref[...], g_ref[...], sh_ref[...], sc_ref[...]).astype(BF16)

    h = h_scr[...]
    a = jnp.dot(h, wa_ref[...], preferred_element_type=F32) + ba_ref[...]
    gate = jnp.dot(h, wg_ref[...], preferred_element_type=F32) + bg_ref[...]
    o_ref[...] = a * _sigmoid(gate)


def pw1_glu(x2d, g, mod, seq, w_bf16, b, tm=1024, tn=512):
    r, d = x2d.shape
    half = w_bf16.shape[1] // 2
    nj = half // tn
    tiles_per_seq = seq // tm
    b2 = b.reshape(1, 2 * half)
    return pl.pallas_call(
        _glu_kernel,
        grid=(r // tm, nj),
        in_specs=[
            pl.BlockSpec((tm, d), lambda i, j: (i, 0)),
            pl.BlockSpec((1, d), lambda i, j: (0, 0)),
            pl.BlockSpec((None, 1, d), lambda i, j: (i // tiles_per_seq, 0, 0)),
            pl.BlockSpec((None, 1, d), lambda i, j: (i // tiles_per_seq, 0, 1)),
            pl.BlockSpec((d, tn), lambda i, j: (0, j)),
            pl.BlockSpec((d, tn), lambda i, j: (0, j + nj)),
            pl.BlockSpec((1, tn), lambda i, j: (0, j)),
            pl.BlockSpec((1, tn), lambda i, j: (0, j + nj)),
        ],
        out_specs=pl.BlockSpec((tm, tn), lambda i, j: (i, j)),
        out_shape=jax.ShapeDtypeStruct((r, half), F32),
        scratch_shapes=[pltpu.VMEM((tm, d), BF16)],
        compiler_params=_cparams(("parallel", "arbitrary")),
        name="pw1_glu",
    )(x2d, g.reshape(1, d), mod, mod, w_bf16, w_bf16, b2, b2)


def _proj_res_kernel(*refs, n_in, splits):
    y_refs = refs[:n_in]
    w_ref, b_ref, x_ref, gt_ref, o_ref = refs[n_in:]
    acc = None
    for y_ref, (lo, hi) in zip(y_refs, splits):
        part = jnp.dot(y_ref[...], w_ref[lo:hi, :], preferred_element_type=F32)
        acc = part if acc is None else acc + part
    o_ref[...] = x_ref[...] + gt_ref[...] * (acc + b_ref[...])


def proj_residual(ys, w_bf16, b, x2d, mod, gate_chunk, seq, tm=512):
    r, d = x2d.shape
    kdim, n = w_bf16.shape
    tn = n
    tiles_per_seq = seq // tm
    splits, lo = [], 0
    for y in ys:
        splits.append((lo, lo + y.shape[1]))
        lo += y.shape[1]
    assert lo == kdim
    chunks_per_d = d // tn
    kern = functools.partial(_proj_res_kernel, n_in=len(ys), splits=tuple(splits))
    return pl.pallas_call(
        kern,
        grid=(r // tm, n // tn),
        in_specs=[pl.BlockSpec((tm, y.shape[1]), lambda i, j: (i, 0)) for y in ys] + [
            pl.BlockSpec((kdim, tn), lambda i, j: (0, j)),
            pl.BlockSpec((1, tn), lambda i, j: (0, j)),
            pl.BlockSpec((tm, tn), lambda i, j: (i, j)),
            pl.BlockSpec((None, 1, tn),
                         lambda i, j: (i // tiles_per_seq, 0, gate_chunk * chunks_per_d + j)),
        ],
        out_specs=pl.BlockSpec((tm, tn), lambda i, j: (i, j)),
        out_shape=jax.ShapeDtypeStruct((r, n), F32),
        compiler_params=_cparams(("parallel", "parallel")),
        name="proj_residual",
    )(*ys, w_bf16, b.reshape(1, n), x2d, mod)


def _na_kernel(q_ref, k_ref, v_ref, kc_ref, vc_ref, bias_ref, o_ref, *, scale, rows, kh):
    i = pl.program_id(2)
    nt = (((1,), (1,)), ((), ()))
    kc, vc = kc_ref[...], vc_ref[...]
    s_ctx = lax.dot_general(q_ref[...], kc, nt, preferred_element_type=F32) * scale
    def scores(rq):
        r = i * NA_QROWS + rq
        rs = jnp.clip(r - kh // 2, 0, rows - kh)
        start = pl.multiple_of(rs * GRID_W, GRID_W)
        qs = slice(rq * GRID_W, (rq + 1) * GRID_W)
        kw = k_ref[pl.ds(start, kh * GRID_W), :]
        s = lax.dot_general(q_ref[qs, :], kw, nt, preferred_element_type=F32) * scale + bias_ref[r - rs]
        return s, start

    def finish(rq, s, start):
        qs = slice(rq * GRID_W, (rq + 1) * GRID_W)
        vw = v_ref[pl.ds(start, kh * GRID_W), :]
        sc = s_ctx[qs, :]
        m = jnp.maximum(jnp.max(s, axis=-1, keepdims=True), jnp.max(sc, axis=-1, keepdims=True))
        p = jnp.exp(s - m)
        pc = jnp.exp(sc - m)
        l = jnp.sum(p, axis=-1, keepdims=True) + jnp.sum(pc, axis=-1, keepdims=True)
        o = (jnp.dot(p.astype(BF16), vw, preferred_element_type=F32)
             + jnp.dot(pc.astype(BF16), vc, preferred_element_type=F32))
        o_ref[qs, :] = (o / l).astype(o_ref.dtype)

    pending = [scores(rq) for rq in range(NA_LOOKAHEAD)]
    for rq in range(NA_QROWS):
        if rq + NA_LOOKAHEAD < NA_QROWS:
            pending.append(scores(rq + NA_LOOKAHEAD))
        finish(rq, *pending.pop(0))


def na_bias_table(rpb, rows):
    heads = rpb.shape[0]
    kh = min(NA_KH, rows)
    c = np.arange(GRID_W)
    dc = c[None, :] - c[:, None]
    col_start = np.clip(c - NA_KW // 2, 0, GRID_W - NA_KW)
    col_valid = (c[None, :] >= col_start[:, None]) & (c[None, :] < col_start[:, None] + NA_KW)
    col_idx = np.clip(dc + NA_KW - 1, 0, 2 * NA_KW - 2)
    toep = jnp.take(rpb, jnp.asarray(col_idx.reshape(-1)), axis=2).reshape(heads, 2 * NA_KH - 1, GRID_W, GRID_W)
    toep = jnp.where(jnp.asarray(col_valid)[None, None], toep, NEG).transpose(0, 2, 1, 3)
    tables = []
    for delta in range(kh):
        lo = NA_KH - 1 - delta
        tables.append(toep[:, :, lo:lo + kh, :].reshape(heads, GRID_W, kh * GRID_W))
    return jnp.stack(tables, axis=1)


def na_attention(qkv, qkv_ctx, bias, heads, q_blk0, k_blk0, v_blk0):
    b, s, _ = qkv.shape
    n_ctx = qkv_ctx.shape[1]
    rows = s // GRID_W
    kh = min(NA_KH, rows)
    tq = NA_QROWS * GRID_W
    kern = functools.partial(_na_kernel, scale=HEAD_DIM ** -0.5, rows=rows, kh=kh)
    return pl.pallas_call(
        kern,
        grid=(b, heads, rows // NA_QROWS),
        in_specs=[
            pl.BlockSpec((None, tq, HEAD_DIM), lambda bi, h, i: (bi, i, q_blk0 + h)),
            pl.BlockSpec((None, s, HEAD_DIM), lambda bi, h, i: (bi, 0, k_blk0 + h)),
            pl.BlockSpec((None, s, HEAD_DIM), lambda bi, h, i: (bi, 0, v_blk0 + h)),
            pl.BlockSpec((None, n_ctx, HEAD_DIM), lambda bi, h, i: (bi, 0, k_blk0 + h)),
            pl.BlockSpec((None, n_ctx, HEAD_DIM), lambda bi, h, i: (bi, 0, v_blk0 + h)),
            pl.BlockSpec((None, kh, GRID_W, kh * GRID_W), lambda bi, h, i: (h, 0, 0, 0)),
        ],
        out_specs=pl.BlockSpec((None, tq, HEAD_DIM), lambda bi, h, i: (bi, i, h)),
        out_shape=jax.ShapeDtypeStruct((b, s, heads * HEAD_DIM), BF16),
        compiler_params=_cparams(("parallel", "parallel", "arbitrary")),
        name="na_attention",
    )(qkv, qkv, qkv, qkv_ctx, qkv_ctx, bias)


def _band_kernel(q_ref, k_ref, v_ref, kc_ref, vc_ref, sink_ref, o_ref, kcat, vcat, *, scale, blk, group, seq):
    kvh = pl.program_id(1)
    n = pl.program_id(2)
    n_win = 3 * blk
    start = pl.multiple_of(jnp.clip((n - 1) * blk, 0, seq - n_win), blk)
    kcat[0:n_win, :] = k_ref[pl.ds(start, n_win), :]
    vcat[0:n_win, :] = v_ref[pl.ds(start, n_win), :]
    kcat[n_win:, :] = kc_ref[...]
    vcat[n_win:, :] = vc_ref[...]
    nk = kcat.shape[0]
    qpos = n * blk + lax.broadcasted_iota(jnp.int32, (blk, nk), 0)
    col = lax.broadcasted_iota(jnp.int32, (blk, nk), 1)
    kpos = start + col
    valid = (col >= n_win) | (jnp.abs(qpos - kpos) <= SW_WINDOW)
    kk, vv = kcat[...], vcat[...]
    def scores(g):
        hs = slice(g * HEAD_DIM, (g + 1) * HEAD_DIM)
        return lax.dot_general(q_ref[:, hs], kk, (((1,), (1,)), ((), ())), preferred_element_type=F32) * scale

    def finish(g, s):
        s = jnp.where(valid, s, NEG)
        sink = sink_ref[kvh * group + g]
        m = jnp.maximum(jnp.max(s, axis=-1, keepdims=True), sink)
        p = jnp.exp(s - m)
        l = jnp.sum(p, axis=-1, keepdims=True) + jnp.exp(sink - m)
        o = jnp.dot(p.astype(BF16), vv, preferred_element_type=F32) / l
        o_ref[:, g * HEAD_DIM:(g + 1) * HEAD_DIM] = o.astype(o_ref.dtype)

    pending = [scores(g) for g in range(ATTN_LOOKAHEAD)]
    for g in range(group):
        if g + ATTN_LOOKAHEAD < group:
            pending.append(scores(g + ATTN_LOOKAHEAD))
        finish(g, pending.pop(0))


def band_attention(qkv, qkv_ctx, sinks, q_heads, kv_heads, q_col0, k_col0, v_col0, blk=128):
    b, s, _ = qkv.shape
    n_ctx = qkv_ctx.shape[1]
    group = q_heads // kv_heads
    gw = group * HEAD_DIM
    nk = 3 * blk + n_ctx
    kern = functools.partial(_band_kernel, scale=HEAD_DIM ** -0.5, blk=blk, group=group, seq=s)
    return pl.pallas_call(
        kern,
        grid=(b, kv_heads, s // blk),
        in_specs=[
            pl.BlockSpec((None, blk, gw), lambda bi, kv, n: (bi, n, q_col0 // gw + kv)),
            pl.BlockSpec((None, s, HEAD_DIM), lambda bi, kv, n: (bi, 0, k_col0 // HEAD_DIM + kv)),
            pl.BlockSpec((None, s, HEAD_DIM), lambda bi, kv, n: (bi, 0, v_col0 // HEAD_DIM + kv)),
            pl.BlockSpec((None, n_ctx, HEAD_DIM), lambda bi, kv, n: (bi, 0, k_col0 // HEAD_DIM + kv)),
            pl.BlockSpec((None, n_ctx, HEAD_DIM), lambda bi, kv, n: (bi, 0, v_col0 // HEAD_DIM + kv)),
            pl.BlockSpec(memory_space=pltpu.SMEM),
        ],
        out_specs=pl.BlockSpec((None, blk, gw), lambda bi, kv, n: (bi, n, kv)),
        out_shape=jax.ShapeDtypeStruct((b, s, q_heads * HEAD_DIM), BF16),
        scratch_shapes=[pltpu.VMEM((nk, HEAD_DIM), BF16), pltpu.VMEM((nk, HEAD_DIM), BF16)],
        compiler_params=_cparams(("parallel", "parallel", "arbitrary")),
        name="band_attention",
    )(qkv, qkv, qkv, qkv_ctx, qkv_ctx, sinks)


def _quantize(v):
    amax = jnp.maximum(jnp.max(jnp.abs(v)), F8_TINY)
    return (v * (F8_MAX / amax)).astype(F8), amax * (1.0 / F8_MAX)


def _quantize_cols(w):
    amax = jnp.maximum(jnp.max(jnp.abs(w), axis=0, keepdims=True), F8_TINY)
    return (w * (F8_MAX / amax)).astype(F8), amax * (1.0 / F8_MAX)


def _ffn_kernel(x_ref, g_ref, sh_ref, sc_ref, gt_ref, wg_ref, wu_ref, wd_ref, o_ref, h_scr):
    f = pl.program_id(1)

    @pl.when(f == 0)
    def _():
        h_scr[...] = _norm_mod(x_ref[...], g_ref[...], sh_ref[...], sc_ref[...]).astype(BF16)
        o_ref[...] = jnp.zeros_like(o_ref)

    h = h_scr[...]
    gate = jnp.dot(h, wg_ref[...], preferred_element_type=F32)
    up = jnp.dot(h, wu_ref[...], preferred_element_type=F32)
    a = (_silu(gate) * up).astype(BF16)
    for c0 in range(0, o_ref.shape[1], FFN_OUT_CHUNK):
        cs = slice(c0, c0 + FFN_OUT_CHUNK)
        o_ref[:, cs] += jnp.dot(a, wd_ref[:, cs], preferred_element_type=F32)

    @pl.when(f == pl.num_programs(1) - 1)
    def _():
        o_ref[...] = x_ref[...] + gt_ref[...] * o_ref[...]


def dense_ffn(x2d, g, mod, seq, wg, wu, wd, tm=1024, tf=512):
    r, d = x2d.shape
    dff = wg.shape[1]
    tiles_per_seq = seq // tm
    row = lambda i, f: i // tiles_per_seq
    return pl.pallas_call(
        _ffn_kernel,
        grid=(r // tm, dff // tf),
        in_specs=[
            pl.BlockSpec((tm, d), lambda i, f: (i, 0), pipeline_mode=pl.Buffered(1)),
            pl.BlockSpec((1, d), lambda i, f: (0, 0)),
            pl.BlockSpec((None, 1, d), lambda i, f: (row(i, f), 0, 3)),
            pl.BlockSpec((None, 1, d), lambda i, f: (row(i, f), 0, 4)),
            pl.BlockSpec((None, 1, d), lambda i, f: (row(i, f), 0, 5)),
            pl.BlockSpec((d, tf), lambda i, f: (0, f)),
            pl.BlockSpec((d, tf), lambda i, f: (0, f)),
            pl.BlockSpec((tf, d), lambda i, f: (f, 0)),
        ],
        out_specs=pl.BlockSpec((tm, d), lambda i, f: (i, 0)),
        out_shape=jax.ShapeDtypeStruct((r, d), F32),
        scratch_shapes=[pltpu.VMEM((tm, d), BF16)],
        compiler_params=_cparams(("parallel", "arbitrary")),
        name="dense_ffn",
    )(x2d, g.reshape(1, d), mod, mod, mod, wg, wu, wd)


def _conv_kernel(prev_ref, cur_ref, next_ref, w_ref, b_ref, g_ref, beta_ref, o_ref, ubuf, cbuf, *, ts, rb, cb):
    i = pl.program_id(1)
    last = pl.num_programs(1) - 1
    d = cur_ref.shape[1]
    zeros = jnp.zeros((CONV_HALO, d), F32)

    @pl.when(i == 0)
    def _():
        ubuf[0, 0:CONV_HALO, :] = zeros

    @pl.when(i > 0)
    def _():
        ubuf[0, 0:CONV_HALO, :] = prev_ref[...]

    ubuf[0, CONV_HALO:CONV_HALO + ts, :] = cur_ref[...]

    @pl.when(i == last)
    def _():
        ubuf[0, CONV_HALO + ts:, :] = zeros

    @pl.when(i < last)
    def _():
        ubuf[0, CONV_HALO + ts:, :] = next_ref[...]

    off = CONV_HALO - CONV_WIDTH // 2
    rows_read = ts + (off + CONV_WIDTH - 1) // SUBLANES * SUBLANES
    for s in range(1, SUBLANES):
        for c0 in range(0, d, CONV_COPY_COLS):
            for a in range(0, rows_read, CONV_COPY_ROWS):
                n = min(CONV_COPY_ROWS, rows_read - a)
                ubuf[s, a:a + n, c0:c0 + CONV_COPY_COLS] = ubuf[0, a + s:a + s + n, c0:c0 + CONV_COPY_COLS]

    for c0 in range(0, d, cb):
        wt = w_ref[:, c0:c0 + cb]
        bias = b_ref[:, c0:c0 + cb]

        def row_body(rt, carry, c0=c0, wt=wt, bias=bias):
            r0 = pl.multiple_of(rt * rb, rb)
            acc = jnp.zeros((rb, cb), F32) + bias
            for k in range(CONV_WIDTH):
                s, q = (off + k) % SUBLANES, (off + k) // SUBLANES
                acc = acc + ubuf[s, pl.ds(r0 + q * SUBLANES, rb), c0:c0 + cb] * wt[k:k + 1, :]
            cbuf[pl.ds(r0, rb), c0:c0 + cb] = acc
            return carry

        lax.fori_loop(0, ts // rb, row_body, 0)

    v = cbuf[...]
    mu = jnp.mean(v, axis=-1, keepdims=True)
    cen = v - mu
    var = jnp.mean(cen * cen, axis=-1, keepdims=True)
    y = cen * lax.rsqrt(var + EPS) * g_ref[...] + beta_ref[...]
    o_ref[...] = _silu(y).astype(o_ref.dtype)


def conv_ln_swish(u, w_dw, b_dw, ln_g, ln_b, ts=256, rb=32, cb=256):
    b, s, d = u.shape
    hb = ts // CONV_HALO
    n_halo = s // CONV_HALO
    kern = functools.partial(_conv_kernel, ts=ts, rb=rb, cb=cb)
    return pl.pallas_call(
        kern,
        grid=(b, s // ts),
        in_specs=[
            pl.BlockSpec((None, CONV_HALO, d), lambda bi, i: (bi, jnp.maximum(i * hb - 1, 0), 0)),
            pl.BlockSpec((None, ts, d), lambda bi, i: (bi, i, 0)),
            pl.BlockSpec((None, CONV_HALO, d), lambda bi, i: (bi, jnp.minimum((i + 1) * hb, n_halo - 1), 0)),
            pl.BlockSpec((CONV_WIDTH, d), lambda bi, i: (0, 0)),
            pl.BlockSpec((1, d), lambda bi, i: (0, 0)),
            pl.BlockSpec((1, d), lambda bi, i: (0, 0)),
            pl.BlockSpec((1, d), lambda bi, i: (0, 0)),
        ],
        out_specs=pl.BlockSpec((None, ts, d), lambda bi, i: (bi, i, 0)),
        out_shape=jax.ShapeDtypeStruct((b, s, d), BF16),
        scratch_shapes=[pltpu.VMEM((SUBLANES, ts + 2 * CONV_HALO, d), F32), pltpu.VMEM((ts, d), F32)],
        compiler_params=_cparams(("parallel", "arbitrary")),
        name="conv_ln_swish",
    )(u, u, u, w_dw, b_dw.reshape(1, d), ln_g.reshape(1, d), ln_b.reshape(1, d))


def _bf16_bits(v):
    b = pltpu.bitcast(v, jnp.uint32)
    return (b + jnp.uint32(0x7FFF) + ((b >> 16) & jnp.uint32(1))) >> 16


def _router_kernel(x_ref, g_ref, sh_ref, sc_ref, wh_ref, wl_ref, hp_ref, idx_ref, wt_ref, *, n_experts):
    h = _norm_mod(x_ref[...], g_ref[...], sh_ref[...], sc_ref[...])
    half = h.shape[1] // 2
    hp_ref[...] = (_bf16_bits(h[:, half:]) << 16) | _bf16_bits(h[:, :half])
    h_hi = h.astype(BF16)
    h_lo = (h - h_hi.astype(F32)).astype(BF16)
    wh, wl = wh_ref[...], wl_ref[...]
    logits = (jnp.dot(h_hi, wh, preferred_element_type=F32)
              + (jnp.dot(h_hi, wl, preferred_element_type=F32) + jnp.dot(h_lo, wh, preferred_element_type=F32)))
    lane = lax.broadcasted_iota(jnp.int32, logits.shape, 1)
    big = logits.shape[1]
    logits = jnp.where(lane < n_experts, logits, -jnp.inf)
    m1 = jnp.max(logits, axis=-1, keepdims=True)
    i1 = jnp.min(jnp.where(logits == m1, lane, big), axis=-1, keepdims=True)
    rest = jnp.where(lane == i1, -jnp.inf, logits)
    m2 = jnp.max(rest, axis=-1, keepdims=True)
    i2 = jnp.min(jnp.where(rest == m2, lane, big), axis=-1, keepdims=True)
    e = jnp.exp(m2 - m1)
    w1 = 1.0 / (1.0 + e)
    w2 = e / (1.0 + e)
    idx_ref[...] = jnp.where(lane == 0, i1, jnp.where(lane == 1, i2, 0))
    wt_ref[...] = jnp.where(lane == 0, w1, jnp.where(lane == 1, w2, 0.0))


def router(x2d, g, mod, seq, w_router, tm=512):
    r, d = x2d.shape
    n_experts = w_router.shape[1]
    lanes = 128
    wpad = jnp.zeros((d, lanes), F32).at[:, :n_experts].set(w_router)
    wh = wpad.astype(BF16)
    wl = (wpad - wh.astype(F32)).astype(BF16)
    tiles_per_seq = seq // tm
    kern = functools.partial(_router_kernel, n_experts=n_experts)
    hp, idx, wt = pl.pallas_call(
        kern,
        grid=(r // tm,),
        in_specs=[
            pl.BlockSpec((tm, d), lambda i: (i, 0)),
            pl.BlockSpec((1, d), lambda i: (0, 0)),
            pl.BlockSpec((None, 1, d), lambda i: (i // tiles_per_seq, 0, 3)),
            pl.BlockSpec((None, 1, d), lambda i: (i // tiles_per_seq, 0, 4)),
            pl.BlockSpec((d, lanes), lambda i: (0, 0)),
            pl.BlockSpec((d, lanes), lambda i: (0, 0)),
        ],
        out_specs=[
            pl.BlockSpec((tm, d // 2), lambda i: (i, 0)),
            pl.BlockSpec((tm, lanes), lambda i: (i, 0)),
            pl.BlockSpec((tm, lanes), lambda i: (i, 0)),
        ],
        out_shape=[
            jax.ShapeDtypeStruct((r, d // 2), jnp.uint32),
            jax.ShapeDtypeStruct((r, lanes), jnp.int32),
            jax.ShapeDtypeStruct((r, lanes), F32),
        ],
        compiler_params=_cparams(("parallel",)),
        name="router",
    )(x2d, g.reshape(1, d), mod, mod, wh, wl)
    return hp, idx[:, :TOP_K], wt[:, :TOP_K]


def _issue_rows(table_ref, idx_ref, idx_base, dst_of_row, sem, row0, n_rows):
    def body(g, carry):
        base = pl.multiple_of(row0 + g * DMA_UNROLL, DMA_UNROLL)
        for u in range(DMA_UNROLL):
            r = base + u
            pltpu.make_async_copy(table_ref.at[pl.ds(idx_ref[idx_base + r], 1)], dst_of_row(r), sem).start()
        return carry

    lax.fori_loop(0, n_rows // DMA_UNROLL, body, 0)


def _expert_kernel(te_ref, tr_ref, nt_ref, src_ref, hp_ref, wg_ref, wu_ref, wd_ref, o_ref, xbuf, x_scr, xs_scr,
                   sem, *, tm, rows_per_step):
    i = pl.program_id(0)
    f = pl.program_id(1)
    n_live = nt_ref[0]
    live = i < n_live

    @pl.when(jnp.logical_not(live) & (f == 0))
    def _():
        o_ref[...] = jnp.zeros_like(o_ref)

    @pl.when((i == 0) & (f == 0))
    def _():
        _issue_rows(hp_ref, src_ref, 0, lambda r: xbuf.at[pl.ds(r, 1)], sem, 0, tm)

    @pl.when(live & (f == 0))
    def _():
        pltpu.make_async_copy(hp_ref.at[pl.ds(0, tm)], xbuf, sem).wait()
        xp = xbuf[...]
        lo = pltpu.bitcast(xp << 16, F32)
        hi = pltpu.bitcast(xp & jnp.uint32(0xFFFF0000), F32)
        half = xp.shape[1]
        x_inv = jnp.maximum(jnp.maximum(jnp.max(jnp.abs(lo)), jnp.max(jnp.abs(hi))), F8_TINY) * (1.0 / F8_MAX)
        x_scr[:, :half] = (lo * (1.0 / x_inv)).astype(F8)
        x_scr[:, half:] = (hi * (1.0 / x_inv)).astype(F8)
        xs_scr[0] = x_inv
        o_ref[...] = jnp.zeros_like(o_ref)

    @pl.when((i + 1 < n_live) & (f < tm // rows_per_step))
    def _():
        _issue_rows(hp_ref, src_ref, (i + 1) * tm, lambda r: xbuf.at[pl.ds(r, 1)], sem, f * rows_per_step,
                    rows_per_step)

    def swiglu_rows(rs, wg, wu, wd):
        (wg8, sg), (wu8, su), (wd8, sd) = wg, wu, wd
        x = x_scr[rs, :]
        x_inv = xs_scr[0]
        gate = jnp.dot(x, wg8, preferred_element_type=F32) * (sg * x_inv)
        up = jnp.dot(x, wu8, preferred_element_type=F32) * (su * x_inv)
        a, a_inv = _quantize(_silu(gate) * up)
        for c0 in range(0, o_ref.shape[1], FFN_OUT_CHUNK):
            cs = slice(c0, c0 + FFN_OUT_CHUNK)
            o_ref[rs, cs] += jnp.dot(a, wd8[:, cs], preferred_element_type=F32) * (sd[:, cs] * a_inv)

    rows_valid = tr_ref[i]
    sparse_rows = EXPERT_SPARSE_BLOCKS * EXPERT_ROW_BLOCK

    @pl.when(live & (rows_valid > sparse_rows))
    def _():
        swiglu_rows(slice(0, tm), _quantize_cols(wg_ref[...]), _quantize_cols(wu_ref[...]),
                    _quantize_cols(wd_ref[...]))

    @pl.when(live & (rows_valid <= sparse_rows))
    def _():
        wg, wu, wd = _quantize_cols(wg_ref[...]), _quantize_cols(wu_ref[...]), _quantize_cols(wd_ref[...])
        for r0 in range(0, sparse_rows, EXPERT_ROW_BLOCK):
            @pl.when(r0 < rows_valid)
            def _(r0=r0):
                swiglu_rows(slice(r0, r0 + EXPERT_ROW_BLOCK), wg, wu, wd)


def expert_ffn(hp, src, tile_expert, tile_rows, n_tiles, wg, wu, wd, tm=1024, tf=512):
    half = hp.shape[1]
    r = src.shape[0]
    d = 2 * half
    dff = wg.shape[2]
    nf = dff // tf
    fetch_steps = next(n for n in range(min(nf, tm // DMA_UNROLL), 0, -1) if (tm // DMA_UNROLL) % n == 0)
    rows_per_step = tm // fetch_steps

    def live_tile(i, nt):
        return jnp.minimum(i, nt[0] - 1)

    def wf(i, f, nt):
        return jnp.where(i < nt[0], f, nf - 1)

    kern = functools.partial(_expert_kernel, tm=tm, rows_per_step=rows_per_step)
    return pl.pallas_call(
        kern,
        grid_spec=pltpu.PrefetchScalarGridSpec(
            num_scalar_prefetch=4,
            grid=(r // tm, nf),
            in_specs=[
                pl.BlockSpec(memory_space=pl.ANY),
                pl.BlockSpec((None, d, tf), lambda i, f, te, tr, nt, src: (te[live_tile(i, nt)], 0, wf(i, f, nt))),
                pl.BlockSpec((None, d, tf), lambda i, f, te, tr, nt, src: (te[live_tile(i, nt)], 0, wf(i, f, nt))),
                pl.BlockSpec((None, tf, d), lambda i, f, te, tr, nt, src: (te[live_tile(i, nt)], wf(i, f, nt), 0)),
            ],
            out_specs=pl.BlockSpec((tm, d), lambda i, f, te, tr, nt, src: (i, 0)),
            scratch_shapes=[pltpu.VMEM((tm, half), jnp.uint32), pltpu.VMEM((tm, d), F8),
                            pltpu.SMEM((1,), F32), pltpu.SemaphoreType.DMA(())],
        ),
        out_shape=jax.ShapeDtypeStruct((r, d), F32),
        compiler_params=_cparams(("arbitrary", "arbitrary")),
        name="expert_ffn",
    )(tile_expert, tile_rows, n_tiles, src, hp, wg, wu, wd)


def _combine_kernel(pos_ref, y_ref, x_ref, w_ref, gt_ref, fg_ref, o_ref, buf, sem, *, tt):
    i = pl.program_id(0)
    slot = i % 2

    def fetch(tile, dst_slot):
        for k in range(TOP_K):
            _issue_rows(y_ref, pos_ref, tile * tt * TOP_K + k * tt, lambda r, k=k: buf.at[dst_slot, k, pl.ds(r, 1)],
                        sem.at[dst_slot], 0, tt)

    @pl.when(i == 0)
    def _():
        fetch(0, 0)

    @pl.when(i + 1 < pl.num_programs(0))
    def _():
        fetch(i + 1, 1 - slot)

    for k in range(TOP_K):
        pltpu.make_async_copy(y_ref.at[pl.ds(0, tt)], buf.at[slot, k], sem.at[slot]).wait()
    w = w_ref[...]
    y = buf[slot, 0] * w[:, 0:1]
    for k in range(1, TOP_K):
        y = y + buf[slot, k] * w[:, k:k + 1]
    x = x_ref[...] + gt_ref[...] * y
    ms = jnp.mean(x * x, axis=-1, keepdims=True)
    o_ref[...] = x * lax.rsqrt(ms + EPS) * fg_ref[...]


def combine_final(y_sorted, pos_kmajor, top_w, x2d, mod, seq, final_g, tt=256):
    r, d = x2d.shape
    tiles_per_seq = seq // tt
    kern = functools.partial(_combine_kernel, tt=tt)
    return pl.pallas_call(
        kern,
        grid_spec=pltpu.PrefetchScalarGridSpec(
            num_scalar_prefetch=1,
            grid=(r // tt,),
            in_specs=[
                pl.BlockSpec(memory_space=pl.ANY),
                pl.BlockSpec((tt, d), lambda i, pos: (i, 0)),
                pl.BlockSpec((tt, TOP_K), lambda i, pos: (i, 0)),
                pl.BlockSpec((None, 1, d), lambda i, pos: (i // tiles_per_seq, 0, 5)),
                pl.BlockSpec((1, d), lambda i, pos: (0, 0)),
            ],
            out_specs=pl.BlockSpec((tt, d), lambda i, pos: (i, 0)),
            scratch_shapes=[pltpu.VMEM((2, TOP_K, tt, d), F32), pltpu.SemaphoreType.DMA((2,))],
        ),
        out_shape=jax.ShapeDtypeStruct((r, d), F32),
        compiler_params=_cparams(("arbitrary",)),
        name="combine_final",
    )(pos_kmajor, y_sorted, x2d, top_w, mod, final_g.reshape(1, d))


def dispatch_plan(top_i, n_experts, tm, tt):
    t = top_i.shape[0]
    flat_e = top_i.reshape(-1)
    onehot = (flat_e[:, None] == jnp.arange(n_experts)[None, :]).astype(jnp.int32)
    rank = jnp.cumsum(onehot, axis=0) - onehot
    counts = jnp.sum(onehot, axis=0)
    tiles = (counts + tm - 1) // tm
    tile_end = jnp.cumsum(tiles)
    group_start = (tile_end - tiles) * tm
    pos = jnp.sum(onehot * (group_start[None, :] + rank), axis=1)
    n_rows = (t * TOP_K // tm + n_experts) * tm
    src = jnp.zeros((n_rows,), jnp.int32).at[pos].set(jnp.arange(t * TOP_K, dtype=jnp.int32) // TOP_K)
    tile_ids = jnp.arange(n_rows // tm)
    tile_expert = jnp.minimum(jnp.sum(tile_ids[:, None] >= tile_end[None, :], axis=1), n_experts - 1)
    tile_in_group = tile_ids - (tile_end - tiles)[tile_expert]
    tile_rows = jnp.where(tile_ids < tile_end[-1], jnp.clip(counts[tile_expert] - tile_in_group * tm, 0, tm), 0)
    pos_kmajor = pos.astype(jnp.int32).reshape(t // tt, tt, TOP_K).transpose(0, 2, 1).reshape(-1)
    return src, pos_kmajor, tile_expert.astype(jnp.int32), tile_rows.astype(jnp.int32), \
        tile_end[-1:].astype(jnp.int32)


def rope_head_order(w_cols):
    d, n = w_cols.shape
    quarter = HEAD_DIM // 4
    q = w_cols.reshape(d, n // HEAD_DIM, 4, quarter)
    return jnp.stack([q[:, :, 0], q[:, :, 2], q[:, :, 1], q[:, :, 3]], axis=2).reshape(d, n)


def rope_tables(seq):
    t = np.arange(seq)
    row, col = t // GRID_W, t % GRID_W
    quarter = HEAD_DIM // 4
    inv = ROPE_BASE ** (-np.arange(0, 2 * quarter, 2, dtype=np.float32) / (2 * quarter))
    lane = np.arange(HEAD_DIM)
    by_row = (lane // quarter) % 2 == 0
    pos = np.where(by_row[None, :], row[:, None], col[:, None]).astype(np.float32)
    ang = jnp.asarray(pos) * jnp.asarray(inv[lane % quarter])[None, :]
    cos, sin = jnp.cos(ang), jnp.sin(ang)
    return cos, jnp.where(jnp.asarray(lane < HEAD_DIM // 2)[None, :], -sin, sin)


def kernel(x, c, ctx, c_ctx, ada_w, ada_b, norm_g, final_g, ab_w_in, ab_rpb, ab_sinks, ab_w_out,
           ffn_w_gate, ffn_w_up, ffn_w_down, conv_w_pw1, conv_b_pw1, conv_w_dw, conv_b_dw,
           conv_ln_g, conv_ln_b, conv_w_pw2, conv_b_pw2, moe_w_router, moe_w_gate, moe_w_up, moe_w_down):
    b, s, d = x.shape
    n_ctx = ctx.shape[1]
    depth = ada_w.shape[0]
    assert depth == 2, "layer schedule below is written for one even and one odd layer"
    na_w = ab_rpb.shape[1] * HEAD_DIM
    sw_q = ab_sinks.shape[1] * HEAD_DIM
    in_w = ab_w_in.shape[2]
    sw_kv = (in_w - 3 * na_w - sw_q) // 2
    n_experts = moe_w_router.shape[2]

    mod_rows = 8
    cvec = jnp.zeros((mod_rows, d), F32).at[:b].set(c).at[b].set(c_ctx)
    mods = ada_mod_all(cvec, ada_w, ada_b).reshape(depth, mod_rows, 1, N_MOD * d)

    x2d = x.reshape(b * s, d)
    ctx2d = ctx.reshape(b * n_ctx, d)

    mod0 = mods[0]
    tabs = rope_tables(s)
    rope_lo, rope_hi = 3 * na_w, 3 * na_w + sw_q + sw_kv
    w_in = jnp.concatenate([ab_w_in[0][:, :rope_lo], rope_head_order(ab_w_in[0][:, rope_lo:rope_hi]),
                            ab_w_in[0][:, rope_hi:]], axis=1).astype(BF16)
    tm_in = min(1024, s)
    qkv = in_proj(x2d, norm_g[0, 0], mod0, lambda i: i // (s // tm_in), w_in, tabs, rope_lo, rope_hi, s, tm_in)
    ctx_tabs = tuple(t[:n_ctx] for t in tabs)
    qkv_ctx = in_proj(ctx2d, norm_g[0, 0], mod0, lambda i: b, w_in, ctx_tabs, 0, 0, n_ctx, n_ctx)
    qkv = qkv.reshape(b, s, in_w)
    qkv_ctx = qkv_ctx.reshape(b, n_ctx, in_w)

    na_heads = na_w // HEAD_DIM
    bias = na_bias_table(ab_rpb[0], s // GRID_W)
    y_a = na_attention(qkv, qkv_ctx, bias, na_heads, 0, na_heads, 2 * na_heads)
    y_b = band_attention(qkv, qkv_ctx, ab_sinks[0], sw_q // HEAD_DIM, sw_kv // HEAD_DIM,
                         3 * na_w, 3 * na_w + sw_q, 3 * na_w + sw_q + sw_kv)
    zero_d = jnp.zeros((d,), F32)
    x2d = proj_residual([y_a.reshape(b * s, na_w), y_b.reshape(b * s, sw_q)], ab_w_out[0].astype(BF16),
                        zero_d, x2d, mod0, 2, s)
    x2d = dense_ffn(x2d, norm_g[0, 1], mod0, s, ffn_w_gate[0].astype(BF16), ffn_w_up[0].astype(BF16),
                    ffn_w_down[0].astype(BF16))

    mod1 = mods[1]
    u = pw1_glu(x2d, norm_g[1, 0], mod1, s, conv_w_pw1[0].astype(BF16), conv_b_pw1[0])
    v = conv_ln_swish(u.reshape(b, s, d), conv_w_dw[0], conv_b_dw[0], conv_ln_g[0], conv_ln_b[0])
    x2d = proj_residual([v.reshape(b * s, d)], conv_w_pw2[0].astype(BF16), conv_b_pw2[0], x2d, mod1, 2, s)

    hp, top_i, top_w = router(x2d, norm_g[1, 1], mod1, s, moe_w_router[0])
    tm_e, tt = 1024, 256
    src, pos_kmajor, tile_expert, tile_rows, n_tiles = dispatch_plan(top_i, n_experts, tm_e, tt)
    y_sorted = expert_ffn(hp, src, tile_expert, tile_rows, n_tiles, moe_w_gate[0], moe_w_up[0], moe_w_down[0],
                          tm=tm_e)
    out = combine_final(y_sorted, pos_kmajor, top_w, x2d, mod1, s, final_g, tt=tt)
    return out.reshape(b, s, d)
```
